```python
import math
import jax, jax.numpy as jnp
from jax import lax
import numpy as np

D_MODEL = 1024
BATCH = 2
SEQ = 8192
DEPTH = 2

GRID_W = 64
N_MEM = 256
D_MIX = D_MODEL
EPS = 1e-6

ATTN_HEADS = 8
ATTN_KV_HEADS = 2
ATTN_GROUP = ATTN_HEADS // ATTN_KV_HEADS
HEAD_DIM = 64
ATTN_W = ATTN_HEADS * HEAD_DIM
Q_BLOCK = 128
ROPE_THETA = 10000.0

CONV_CH = D_MIX // 4
CONV_WIDTH = 31

MLSTM_HEADS = 4
MLSTM_HEAD_DIM = 64
MLSTM_W = MLSTM_HEADS * MLSTM_HEAD_DIM
MLSTM_CHUNK = 128
N_DIRS = 2
N_GATES = N_DIRS * 2 * MLSTM_HEADS

XATTN_HEADS = 4
XATTN_HEAD_DIM = D_MODEL // XATTN_HEADS

D_FF = 2816

OFF_Q = 0
OFF_K = OFF_Q + ATTN_W
OFF_V = OFF_K + ATTN_KV_HEADS * HEAD_DIM
OFF_CONV = OFF_V + ATTN_KV_HEADS * HEAD_DIM
OFF_MQ = OFF_CONV + 2 * CONV_CH
OFF_MK = OFF_MQ + MLSTM_W
OFF_MV = OFF_MK + MLSTM_W
OFF_MO = OFF_MV + MLSTM_W
OFF_MG = OFF_MO + MLSTM_W
N_IN = OFF_MG + N_GATES

kernel_name = "hybrid_parallel_attn_conv_mlstm_encoder"


def rms_norm(x, g):
    xf = x.astype(jnp.float32)
    y = xf * lax.rsqrt(jnp.mean(xf * xf, axis=-1, keepdims=True) + EPS)
    return (y * g.astype(jnp.float32)).astype(x.dtype)


def swiglu_ffn(x, g, w13, w2):
    h = rms_norm(x, g)
    a, b = jnp.split(h @ w13, 2, axis=-1)
    return (jax.nn.silu(a) * b) @ w2


def axial_rope_table(seq_len):
    rows = seq_len // GRID_W
    row_idx = jnp.repeat(jnp.arange(rows, dtype=jnp.int32), GRID_W).astype(jnp.float32)
    col_idx = jnp.tile(jnp.arange(GRID_W, dtype=jnp.int32), rows).astype(jnp.float32)
    n_freq = HEAD_DIM // 4
    inv_freq = jnp.float32(ROPE_THETA) ** (-jnp.arange(n_freq, dtype=jnp.float32) / n_freq)
    ang = jnp.concatenate([row_idx[:, None] * inv_freq, col_idx[:, None] * inv_freq], axis=-1)
    return jnp.cos(ang), jnp.sin(ang)


def apply_rope(x, cos, sin):
    xf = x.astype(jnp.float32).reshape(x.shape[:-1] + (x.shape[-1] // 2, 2))
    x0, x1 = xf[..., 0], xf[..., 1]
    c = cos[None, :, None, :]
    s = sin[None, :, None, :]
    out = jnp.stack([x0 * c - x1 * s, x0 * s + x1 * c], axis=-1).reshape(x.shape)
    return out.astype(x.dtype)


def attention_group(hq, hk, hv, q_gain, k_gain, cos, sin):
    B, S, _ = hq.shape
    q = hq.reshape(B, S, ATTN_HEADS, HEAD_DIM)
    k = hk.reshape(B, S, ATTN_KV_HEADS, HEAD_DIM)
    v = hv.reshape(B, S, ATTN_KV_HEADS, HEAD_DIM)
    q = apply_rope(rms_norm(q, q_gain), cos, sin) * (HEAD_DIM ** -0.5)
    k = apply_rope(rms_norm(k, k_gain), cos, sin)
    nb = S // Q_BLOCK
    qb_all = q.reshape(B, nb, Q_BLOCK, ATTN_KV_HEADS, ATTN_GROUP, HEAD_DIM).transpose(1, 0, 2, 3, 4, 5)

    def one_block(qb):
        s = jnp.einsum('bqhgd,bkhd->bhgqk', qb, k, preferred_element_type=jnp.float32)
        p = jax.nn.softmax(s, axis=-1).astype(v.dtype)
        return jnp.einsum('bhgqk,bkhd->bqhgd', p, v)

    o = lax.map(one_block, qb_all)
    return o.transpose(1, 0, 2, 3, 4, 5).reshape(B, S, ATTN_W)


def conv_group(hc, dw_w, dw_b, ln_g, ln_b):
    a, gate = jnp.split(hc, 2, axis=-1)
    u = a * jax.nn.sigmoid(gate)
    pad = CONV_WIDTH // 2
    u = lax.conv_general_dilated(
        u, dw_w[:, None, :].astype(u.dtype), window_strides=(1,), padding=[(pad, pad)],
        dimension_numbers=('NWC', 'WIO', 'NWC'), feature_group_count=CONV_CH) + dw_b
    uf = u.astype(jnp.float32)
    mu = jnp.mean(uf, axis=-1, keepdims=True)
    var = jnp.mean(jnp.square(uf - mu), axis=-1, keepdims=True)
    un = (uf - mu) * lax.rsqrt(var + EPS) * ln_g.astype(jnp.float32) + ln_b.astype(jnp.float32)
    return jax.nn.silu(un).astype(hc.dtype)


def mlstm_scan(q, k, v, i_pre, log_f):
    B, S, H, D = q.shape
    L = MLSTM_CHUNK
    nc = S // L

    def chunk4(t):
        return t.reshape(B, nc, L, H, D).transpose(1, 0, 3, 2, 4)

    def chunk3(t):
        return t.reshape(B, nc, L, H).transpose(1, 0, 3, 2)

    mask = jnp.tril(jnp.ones((L, L), dtype=bool))
    neg_inf = jnp.float32(-jnp.inf)

    def step(carry, inp):
        C, n, m = carry
        qc, kc, vc, ic, fc = inp
        b = jnp.cumsum(fc, axis=-1)
        logw = b[..., :, None] - b[..., None, :] + ic[..., None, :]
        logw = jnp.where(mask, logw, neg_inf)
        m_inter = b + m[..., None]
        m_t = jnp.maximum(jnp.max(logw, axis=-1), m_inter)
        w = jnp.exp(logw - m_t[..., None])
        s = jnp.einsum('bhtd,bhsd->bhts', qc, kc) * w
        inter = jnp.exp(m_inter - m_t)
        num = jnp.einsum('bhts,bhsd->bhtd', s, vc) + inter[..., None] * jnp.einsum('bhed,bhtd->bhte', C, qc)
        den = jnp.sum(s, axis=-1) + inter * jnp.einsum('bhd,bhtd->bht', n, qc)
        h = num / jnp.maximum(jnp.abs(den), jnp.exp(-m_t))[..., None]
        b_end = b[..., -1]
        logw_end = b_end[..., None] - b + ic
        m_new = jnp.maximum(b_end + m, jnp.max(logw_end, axis=-1))
        w_end = jnp.exp(logw_end - m_new[..., None])
        decay = jnp.exp(b_end + m - m_new)
        C_new = decay[..., None, None] * C + jnp.einsum('bhs,bhse,bhsd->bhed', w_end, vc, kc)
        n_new = decay[..., None] * n + jnp.einsum('bhs,bhsd->bhd', w_end, kc)
        return (C_new, n_new, m_new), h

    init = (jnp.zeros((B, H, D, D), jnp.float32), jnp.zeros((B, H, D), jnp.float32),
            jnp.zeros((B, H), jnp.float32))
    _, hs = lax.scan(step, init, (chunk4(q), chunk4(k), chunk4(v), chunk3(i_pre), chunk3(log_f)))
    return hs.transpose(1, 0, 3, 2, 4).reshape(B, S, H, D)


def mlstm_group(hq, hk, hv, ho, hg, gate_b, out_gain):
    B, S, _ = hq.shape
    shp = (B, S, MLSTM_HEADS, MLSTM_HEAD_DIM)
    q = hq.reshape(shp).astype(jnp.float32)
    k = hk.reshape(shp).astype(jnp.float32) * (MLSTM_HEAD_DIM ** -0.5)
    v = hv.reshape(shp).astype(jnp.float32)
    g = hg.astype(jnp.float32).reshape(B, S, N_DIRS, 2, MLSTM_HEADS) + gate_b.astype(jnp.float32)
    i_pre = g[:, :, :, 0, :]
    log_f = jax.nn.log_sigmoid(g[:, :, :, 1, :])
    h_fwd = mlstm_scan(q, k, v, i_pre[:, :, 0], log_f[:, :, 0])
    rev = lambda t: jnp.flip(t, axis=1)
    h_bwd = rev(mlstm_scan(rev(q), rev(k), rev(v), rev(i_pre[:, :, 1]), rev(log_f[:, :, 1])))
    h = rms_norm(h_fwd + h_bwd, out_gain)
    return (jax.nn.sigmoid(ho.astype(jnp.float32)) * h.reshape(B, S, MLSTM_W)).astype(hq.dtype)


def memory_cross_attention(x, mem, x_gain, mem_gain, wq, wkv, q_gain, k_gain, wo):
    B, S, _ = x.shape
    M = mem.shape[1]
    q = (rms_norm(x, x_gain) @ wq).reshape(B, S, XATTN_HEADS, XATTN_HEAD_DIM)
    k, v = jnp.split(rms_norm(mem, mem_gain) @ wkv, 2, axis=-1)
    k = k.reshape(B, M, XATTN_HEADS, XATTN_HEAD_DIM)
    v = v.reshape(B, M, XATTN_HEADS, XATTN_HEAD_DIM)
    q = rms_norm(q, q_gain) * (XATTN_HEAD_DIM ** -0.5)
    k = rms_norm(k, k_gain)
    s = jnp.einsum('bqhd,bmhd->bhqm', q, k, preferred_element_type=jnp.float32)
    p = jax.nn.softmax(s, axis=-1).astype(v.dtype)
    o = jnp.einsum('bhqm,bmhd->bqhd', p, v).reshape(B, S, XATTN_HEADS * XATTN_HEAD_DIM)
    return o @ wo


def setup_inputs(seed: int = 0) -> dict:
    key = jax.random.key(seed)
    ks = iter(jax.random.split(key, 32))
    L = DEPTH

    def nrm(shape, scale):
        return jax.random.normal(next(ks), shape, jnp.float32) * scale

    def gain(shape):
        return 1.0 + nrm(shape, 0.02)

    x = nrm((BATCH, SEQ, D_MODEL), 1.0)
    mem = nrm((BATCH, N_MEM, D_MODEL), 1.0)
    ffn1_norm = gain((L, D_MODEL))
    ffn1_w13 = nrm((L, D_MODEL, 2 * D_FF), D_MODEL ** -0.5)
    ffn1_w2 = nrm((L, D_FF, D_MODEL), D_FF ** -0.5)
    mix_norm = gain((L, D_MODEL))
    w_in = nrm((L, D_MODEL, N_IN), D_MODEL ** -0.5)
    attn_q_norm = gain((L, HEAD_DIM))
    attn_k_norm = gain((L, HEAD_DIM))
    conv_dw_w = nrm((L, CONV_WIDTH, CONV_CH), CONV_WIDTH ** -0.5)
    conv_dw_b = nrm((L, CONV_CH), 0.02)
    conv_ln_g = gain((L, CONV_CH))
    conv_ln_b = nrm((L, CONV_CH), 0.02)
    i_bias = nrm((L, N_DIRS, 1, MLSTM_HEADS), 0.1)
    f_bias = jnp.linspace(3.0, 6.0, MLSTM_HEADS, dtype=jnp.float32) + nrm((L, N_DIRS, 1, MLSTM_HEADS), 0.1)
    mlstm_gate_b = jnp.concatenate([i_bias, f_bias], axis=2)
    mlstm_out_norm = gain((L, MLSTM_HEADS, MLSTM_HEAD_DIM))
    w_out = nrm((L, D_MIX, D_MODEL), D_MIX ** -0.5)
    xattn_norm = gain((L, D_MODEL))
    mem_norm = gain((L, D_MODEL))
    xattn_wq = nrm((L, D_MODEL, XATTN_HEADS * XATTN_HEAD_DIM), D_MODEL ** -0.5)
    xattn_wkv = nrm((L, D_MODEL, 2 * XATTN_HEADS * XATTN_HEAD_DIM), D_MODEL ** -0.5)
    xattn_q_norm = gain((L, XATTN_HEAD_DIM))
    xattn_k_norm = gain((L, XATTN_HEAD_DIM))
    xattn_wo = nrm((L, XATTN_HEADS * XATTN_HEAD_DIM, D_MODEL), (XATTN_HEADS * XATTN_HEAD_DIM) ** -0.5)
    ffn2_norm = gain((L, D_MODEL))
    ffn2_w13 = nrm((L, D_MODEL, 2 * D_FF), D_MODEL ** -0.5)
    ffn2_w2 = nrm((L, D_FF, D_MODEL), D_FF ** -0.5)
    return {
        "x": x, "mem": mem,
        "ffn1_norm": ffn1_norm, "ffn1_w13": ffn1_w13, "ffn1_w2": ffn1_w2,
        "mix_norm": mix_norm, "w_in": w_in,
        "attn_q_norm": attn_q_norm, "attn_k_norm": attn_k_norm,
        "conv_dw_w": conv_dw_w, "conv_dw_b": conv_dw_b, "conv_ln_g": conv_ln_g, "conv_ln_b": conv_ln_b,
        "mlstm_gate_b": mlstm_gate_b, "mlstm_out_norm": mlstm_out_norm,
        "w_out": w_out,
        "xattn_norm": xattn_norm, "mem_norm": mem_norm, "xattn_wq": xattn_wq, "xattn_wkv": xattn_wkv,
        "xattn_q_norm": xattn_q_norm, "xattn_k_norm": xattn_k_norm, "xattn_wo": xattn_wo,
        "ffn2_norm": ffn2_norm, "ffn2_w13": ffn2_w13, "ffn2_w2": ffn2_w2,
    }


def reference(x, mem, ffn1_norm, ffn1_w13, ffn1_w2, mix_norm, w_in, attn_q_norm, attn_k_norm,
              conv_dw_w, conv_dw_b, conv_ln_g, conv_ln_b, mlstm_gate_b, mlstm_out_norm, w_out,
              xattn_norm, mem_norm, xattn_wq, xattn_wkv, xattn_q_norm, xattn_k_norm, xattn_wo,
              ffn2_norm, ffn2_w13, ffn2_w2):
    cos, sin = axial_rope_table(x.shape[1])
    for l in range(DEPTH):
        x = x + 0.5 * swiglu_ffn(x, ffn1_norm[l], ffn1_w13[l], ffn1_w2[l])
        h = rms_norm(x, mix_norm[l]) @ w_in[l]
        y_attn = attention_group(h[..., OFF_Q:OFF_K], h[..., OFF_K:OFF_V], h[..., OFF_V:OFF_CONV],
                                 attn_q_norm[l], attn_k_norm[l], cos, sin)
        y_conv = conv_group(h[..., OFF_CONV:OFF_MQ], conv_dw_w[l], conv_dw_b[l], conv_ln_g[l], conv_ln_b[l])
        y_mlstm = mlstm_group(h[..., OFF_MQ:OFF_MK], h[..., OFF_MK:OFF_MV], h[..., OFF_MV:OFF_MO],
                              h[..., OFF_MO:OFF_MG], h[..., OFF_MG:N_IN], mlstm_gate_b[l], mlstm_out_norm[l])
        x = x + jnp.concatenate([y_attn, y_conv, y_mlstm], axis=-1) @ w_out[l]
        x = x + memory_cross_attention(x, mem, xattn_norm[l], mem_norm[l], xattn_wq[l], xattn_wkv[l],
                                       xattn_q_norm[l], xattn_k_norm[l], xattn_wo[l])
        x = x + 0.5 * swiglu_ffn(x, ffn2_norm[l], ffn2_w13[l], ffn2_w2[l])
    return x
```

```python
import functools

import jax
import jax.numpy as jnp
from jax import lax
from jax.experimental import pallas as pl
from jax.experimental.pallas import tpu as pltpu

F32 = jnp.float32
BF16 = jnp.bfloat16

GRID_W = 64
EPS = 1e-6
ATTN_HEADS = 8
ATTN_KV_HEADS = 2
HEAD_DIM = 64
ATTN_W = ATTN_HEADS * HEAD_DIM
KV_W = ATTN_KV_HEADS * HEAD_DIM
ROPE_THETA = 10000.0
CONV_CH = 256
CONV_WIDTH = 31
CONV_PAD = CONV_WIDTH // 2
MLSTM_HEADS = 4
MLSTM_HEAD_DIM = 64
MLSTM_W = MLSTM_HEADS * MLSTM_HEAD_DIM
MLSTM_CHUNK = 128
N_GATES = 16
XATTN_HEADS = 4
OFF_K = ATTN_W
OFF_V = OFF_K + KV_W
OFF_CONV = OFF_V + KV_W
OFF_MQ = OFF_CONV + 2 * CONV_CH
OFF_MK = OFF_MQ + MLSTM_W
OFF_MV = OFF_MK + MLSTM_W
OFF_MO = OFF_MV + MLSTM_W
OFF_MG = OFF_MO + MLSTM_W

LANES = 128
HALO_ROWS = 16
VMEM_LIMIT = 56 * 1024 * 1024


def _cparams(sem):
    return pltpu.CompilerParams(dimension_semantics=sem, vmem_limit_bytes=VMEM_LIMIT)


def _const_spec(shape):
    nd = len(shape)
    return pl.BlockSpec(shape, lambda *_: (0,) * nd)


def _rms(x, g):
    return x * lax.rsqrt(jnp.mean(x * x, axis=-1, keepdims=True) + EPS) * g


def _group_mean(sq, bd):
    hi = sq.astype(BF16)
    lo = (sq - hi.astype(F32)).astype(BF16)
    return (jnp.dot(hi, bd, preferred_element_type=F32)
            + jnp.dot(lo, bd, preferred_element_type=F32))


def _swap_pairs(x):
    nxt = pltpu.roll(x, LANES - 1, axis=1)
    prv = pltpu.roll(x, 1, axis=1)
    lane = lax.broadcasted_iota(jnp.int32, x.shape, 1)
    return jnp.where((lane & 1) == 0, nxt, prv)


def _sigmoid(x):
    return 1.0 / (1.0 + jnp.exp(-x))


def _ffn_body(x_ref, g_ref, w13_ref, w2_ref, o_ref, *, d_ff, n_chunks):
    x = x_ref[...]
    xb = _rms(x, g_ref[...]).astype(BF16)
    fc = d_ff // n_chunks
    acc = None
    for c in range(n_chunks):
        a = jnp.dot(xb, w13_ref[:, c * fc:(c + 1) * fc], preferred_element_type=F32)
        b = jnp.dot(xb, w13_ref[:, d_ff + c * fc:d_ff + (c + 1) * fc], preferred_element_type=F32)
        act = (a * _sigmoid(a) * b).astype(BF16)
        part = jnp.dot(act, w2_ref[c * fc:(c + 1) * fc, :], preferred_element_type=F32)
        acc = part if acc is None else acc + part
    o_ref[...] = x + 0.5 * acc


def _ffn(x2d, g, w13, w2, *, tm, n_chunks):
    n, d = x2d.shape
    d_ff = w2.shape[0]
    return pl.pallas_call(
        functools.partial(_ffn_body, d_ff=d_ff, n_chunks=n_chunks),
        out_shape=jax.ShapeDtypeStruct((n, d), F32),
        grid=(n // tm,),
        in_specs=[
            pl.BlockSpec((tm, d), lambda i: (i, 0)),
            _const_spec((1, d)),
            pl.BlockSpec((d, 2 * d_ff), lambda i: (0, 0), pipeline_mode=pl.Buffered(1)),
            pl.BlockSpec((d_ff, d), lambda i: (0, 0), pipeline_mode=pl.Buffered(1)),
        ],
        out_specs=pl.BlockSpec((tm, d), lambda i: (i, 0)),
        compiler_params=_cparams(("parallel",)),
        name="swiglu_half_step",
    )(x2d, g, w13, w2)


def _inproj_body(x_ref, g_ref, w_ref, wg_ref, wgt_ref, gb_ref, gbt_ref, qg_ref, kg_ref,
                 cos_ref, sin_ref, bd_ref,
                 q_ref, kp_ref, vp_ref, u_ref, mq_ref, mk_ref, mv_ref, og_ref, gate_ref, gatet_ref):
    xb = _rms(x_ref[0], g_ref[...]).astype(BF16)
    h = jnp.dot(xb, w_ref[...], preferred_element_type=F32)
    bd = bd_ref[...]
    cos = cos_ref[...]
    sin = sin_ref[...]

    hq = h[:, 0:ATTN_W]
    ms = jnp.concatenate([_group_mean(hq[:, 0:256] * hq[:, 0:256], bd),
                          _group_mean(hq[:, 256:512] * hq[:, 256:512], bd)], axis=1)
    qn = hq * lax.rsqrt(ms + EPS) * qg_ref[...]
    for j in range(ATTN_W // LANES):
        z = qn[:, j * LANES:(j + 1) * LANES]
        r = (z * cos + _swap_pairs(z) * sin) * (HEAD_DIM ** -0.5)
        q_ref[0, :, j * LANES:(j + 1) * LANES] = r.astype(BF16)

    hk = h[:, OFF_K:OFF_V]
    msk = _group_mean(hk * hk, bd_ref[0:LANES, 0:LANES])
    kn = hk * lax.rsqrt(msk + EPS) * kg_ref[...]
    kr = kn * cos + _swap_pairs(kn) * sin
    hv = h[:, OFF_V:OFF_CONV]
    lane = lax.broadcasted_iota(jnp.int32, kr.shape, 1)
    first = lane < HEAD_DIM
    for src, dst in ((kr, kp_ref), (hv, vp_ref)):
        a0 = jnp.where(first, src, 0.0)
        b1 = jnp.where(first, 0.0, src)
        dst[0, 0] = a0.astype(BF16)
        dst[0, 1] = pltpu.roll(a0, HEAD_DIM, axis=1).astype(BF16)
        dst[0, 2] = pltpu.roll(b1, HEAD_DIM, axis=1).astype(BF16)
        dst[0, 3] = b1.astype(BF16)

    u_ref[0] = h[:, OFF_CONV:OFF_CONV + CONV_CH] * _sigmoid(h[:, OFF_CONV + CONV_CH:OFF_MQ])

    mq_ref[0] = h[:, OFF_MQ:OFF_MK].astype(BF16)
    mk_ref[0] = (h[:, OFF_MK:OFF_MV] * (MLSTM_HEAD_DIM ** -0.5)).astype(BF16)
    mv_ref[0] = h[:, OFF_MV:OFF_MO].astype(BF16)
    og_ref[0] = _sigmoid(h[:, OFF_MO:OFF_MG])

    def log_gate(pre, is_forget):
        ls = jnp.minimum(pre, 0.0) - jnp.log1p(jnp.exp(-jnp.abs(pre)))
        return jnp.where(is_forget, ls, pre)

    gp = jnp.dot(xb, wg_ref[...], preferred_element_type=F32) + gb_ref[...]
    col = lax.broadcasted_iota(jnp.int32, gp.shape, 1)
    gate_ref[0] = log_gate(gp, ((col >> 2) & 1) == 1)
    gpt = lax.dot_general(wgt_ref[...], xb, (((1,), (1,)), ((), ())),
                          preferred_element_type=F32) + gbt_ref[...]
    row = lax.broadcasted_iota(jnp.int32, gpt.shape, 0)
    gatet_ref[0] = log_gate(gpt, ((row >> 2) & 1) == 1)


def _inproj(x, g, w_main, w_gate, w_gate_t, gate_b, gate_bt, q_gain, k_gain, cos, sin, bd, *, tm):
    b, s, d = x.shape
    tok = lambda width, dt: jax.ShapeDtypeStruct((b, s, width), dt)
    tok_spec = lambda width: pl.BlockSpec((1, tm, width), lambda bi, i: (bi, i, 0))
    pad_spec = pl.BlockSpec((1, 4, tm, LANES), lambda bi, i: (bi, 0, i, 0))
    return pl.pallas_call(
        _inproj_body,
        out_shape=(
            tok(ATTN_W, BF16),
            jax.ShapeDtypeStruct((b, 4, s, LANES), BF16),
            jax.ShapeDtypeStruct((b, 4, s, LANES), BF16),
            tok(CONV_CH, F32),
            tok(MLSTM_W, BF16), tok(MLSTM_W, BF16), tok(MLSTM_W, BF16),
            tok(MLSTM_W, F32),
            tok(LANES, F32),
            jax.ShapeDtypeStruct((b, N_GATES, s), F32),
        ),
        grid=(b, s // tm),
        in_specs=[
            tok_spec(d),
            _const_spec((1, d)),
            _const_spec(w_main.shape),
            _const_spec(w_gate.shape),
            _const_spec(w_gate_t.shape),
            _const_spec((1, LANES)),
            _const_spec((N_GATES, 1)),
            _const_spec((1, ATTN_W)),
            _const_spec((1, LANES)),
            pl.BlockSpec((tm, LANES), lambda bi, i: (i, 0)),
            pl.BlockSpec((tm, LANES), lambda bi, i: (i, 0)),
            _const_spec(bd.shape),
        ],
        out_specs=(
            tok_spec(ATTN_W), pad_spec, pad_spec, tok_spec(CONV_CH),
            tok_spec(MLSTM_W), tok_spec(MLSTM_W), tok_spec(MLSTM_W), tok_spec(MLSTM_W),
            tok_spec(LANES),
            pl.BlockSpec((1, N_GATES, tm), lambda bi, i: (bi, 0, i)),
        ),
        compiler_params=_cparams(("parallel", "parallel")),
        name="input_projection",
    )(x, g, w_main, w_gate, w_gate_t, gate_b, gate_bt, q_gain, k_gain, cos, sin, bd)


def _attn_body(q_ref, k_ref, v_ref, o_ref, *, tk):
    tq = q_ref.shape[1]
    s_len = k_ref.shape[3]
    q2 = jnp.concatenate([q_ref[0, :, 0:LANES], q_ref[0, :, LANES:2 * LANES]], axis=0)

    def step(c, carry):
        start = pl.multiple_of(c * tk, tk)
        out = []
        for j in range(2):
            m, l, acc = carry[j]
            kc = k_ref[0, 0, j, pl.ds(start, tk), :]
            vc = v_ref[0, 0, j, pl.ds(start, tk), :]
            s = lax.dot_general(q2, kc, (((1,), (1,)), ((), ())), preferred_element_type=F32)
            m_new = jnp.maximum(m, jnp.max(s, axis=-1, keepdims=True))
            alpha = jnp.exp(m - m_new)
            p = jnp.exp(s - m_new)
            l_new = alpha * l + jnp.sum(p, axis=-1, keepdims=True)
            acc_new = alpha * acc + jnp.dot(p.astype(BF16), vc, preferred_element_type=F32)
            out.append((m_new, l_new, acc_new))
        return tuple(out)

    init = tuple((jnp.full((2 * tq, 1), -jnp.inf, F32), jnp.zeros((2 * tq, 1), F32),
                  jnp.zeros((2 * tq, LANES), F32)) for _ in range(2))
    (_, l0, a0), (_, l1, a1) = lax.fori_loop(0, s_len // tk, step, init)
    o = a0 / l0 + a1 / l1
    o_ref[0, :, 0:LANES] = o[0:tq].astype(BF16)
    o_ref[0, :, LANES:2 * LANES] = o[tq:2 * tq].astype(BF16)


def _attention(q, kp, vp, *, tq, tk):
    b, s, _ = q.shape
    kp5 = kp.reshape(b, ATTN_KV_HEADS, 2, s, LANES)
    vp5 = vp.reshape(b, ATTN_KV_HEADS, 2, s, LANES)
    kv_spec = pl.BlockSpec((1, 1, 2, s, LANES), lambda bi, g, i: (bi, g, 0, 0, 0))
    return pl.pallas_call(
        functools.partial(_attn_body, tk=tk),
        out_shape=jax.ShapeDtypeStruct((b, s, ATTN_W), BF16),
        grid=(b, ATTN_KV_HEADS, s // tq),
        in_specs=[pl.BlockSpec((1, tq, 2 * LANES), lambda bi, g, i: (bi, i, g)), kv_spec, kv_spec],
        out_specs=pl.BlockSpec((1, tq, 2 * LANES), lambda bi, g, i: (bi, i, g)),
        compiler_params=_cparams(("parallel", "parallel", "arbitrary")),
        name="gqa_attention",
    )(q, kp5, vp5)


def _conv_body(prev_ref, cur_ref, next_ref, w_ref, b_ref, lg_ref, lb_ref, o_ref):
    i = pl.program_id(1)
    ts = cur_ref.shape[1]
    prev = jnp.where(i > 0, prev_ref[0], 0.0)
    nxt = jnp.where(i < pl.num_programs(1) - 1, next_ref[0], 0.0)
    win = jnp.concatenate([prev, cur_ref[0], nxt], axis=0)
    w = w_ref[...]
    acc = jnp.zeros((ts, CONV_CH), F32) + b_ref[...]
    for t in range(CONV_WIDTH):
        off = HALO_ROWS - CONV_PAD + t
        acc = acc + win[off:off + ts, :] * w[t:t + 1, :]
    mu = jnp.mean(acc, axis=-1, keepdims=True)
    cen = acc - mu
    var = jnp.mean(cen * cen, axis=-1, keepdims=True)
    un = cen * lax.rsqrt(var + EPS) * lg_ref[...] + lb_ref[...]
    o_ref[0] = (un * _sigmoid(un)).astype(BF16)


def _conv(u, dw_w, dw_b, ln_g, ln_b, *, ts):
    b, s, c = u.shape
    r = ts // HALO_ROWS
    n_halo = s // HALO_ROWS
    return pl.pallas_call(
        _conv_body,
        out_shape=jax.ShapeDtypeStruct((b, s, c), BF16),
        grid=(b, s // ts),
        in_specs=[
            pl.BlockSpec((1, HALO_ROWS, c), lambda bi, i: (bi, jnp.maximum(i * r - 1, 0), 0)),
            pl.BlockSpec((1, ts, c), lambda bi, i: (bi, i, 0)),
            pl.BlockSpec((1, HALO_ROWS, c), lambda bi, i: (bi, jnp.minimum((i + 1) * r, n_halo - 1), 0)),
            _const_spec((CONV_WIDTH, c)),
            _const_spec((1, c)), _const_spec((1, c)), _const_spec((1, c)),
        ],
        out_specs=pl.BlockSpec((1, ts, c), lambda bi, i: (bi, i, 0)),
        compiler_params=_cparams(("parallel", "parallel")),
        name="conformer_conv",
    )(u, u, u, dw_w, dw_b, ln_g, ln_b)


def _expand_heads(cols):
    rows = cols[0].shape[0]
    hid = lax.broadcasted_iota(jnp.int32, (rows, MLSTM_W), 1) >> 6
    out = jnp.broadcast_to(cols[3], (rows, MLSTM_W))
    for h in (2, 1, 0):
        out = jnp.where(hid == h, cols[h], out)
    return out


def _split3(x):
    a = x.astype(BF16)
    r = x - a.astype(F32)
    b = r.astype(BF16)
    c = (r - b.astype(F32)).astype(BF16)
    return a, b, c


def _mlstm_direction(d, q, k, v, gate, gate_t, c_ref, n_ref, m_ref, bd1):
    L = q.shape[0]
    r_i = lax.broadcasted_iota(jnp.int32, (L, L), 0)
    c_i = lax.broadcasted_iota(jnp.int32, (L, L), 1)
    keep = (c_i <= r_i) if d == 0 else (c_i >= r_i)
    tri = jnp.where(keep, 1.0, 0.0).astype(BF16)
    tri_t = jnp.where((r_i <= c_i) if d == 0 else (r_i >= c_i), 1.0, 0.0).astype(BF16)

    cum_col = sum(jnp.dot(tri, t, preferred_element_type=F32) for t in _split3(gate))
    cum_row = sum(jnp.dot(t, tri_t, preferred_element_type=F32) for t in _split3(gate_t))
    tot = jnp.sum(gate, axis=0, keepdims=True)

    hid = lax.broadcasted_iota(jnp.int32, (L, MLSTM_W), 1) >> 6
    m_prev_l = m_ref[d]
    qf = q.astype(F32)
    kf = k.astype(F32)

    m_t_c, inter_c, rs_c, wend_c, mnew_c, decay_c = [], [], [], [], [], []
    num_intra = jnp.zeros((L, MLSTM_W), F32)
    for h in range(MLSTM_HEADS):
        ji = d * 8 + h
        jf = d * 8 + MLSTM_HEADS + h
        m_prev = m_prev_l[:, h * MLSTM_HEAD_DIM:h * MLSTM_HEAD_DIM + 1]
        b_col = cum_col[:, jf:jf + 1]
        i_col = gate[:, ji:ji + 1]
        b_row = cum_row[jf:jf + 1, :]
        i_row = gate_t[ji:ji + 1, :]
        b_end = tot[:, jf:jf + 1]

        logw = jnp.where(keep, b_col - b_row + i_row, -jnp.inf)
        m_inter = b_col + m_prev
        m_t = jnp.maximum(jnp.max(logw, axis=-1, keepdims=True), m_inter)
        w = jnp.exp(logw - m_t)
        qm = jnp.where(hid == h, qf, 0.0).astype(BF16)
        s = lax.dot_general(qm, k, (((1,), (1,)), ((), ())), preferred_element_type=F32) * w
        r = jnp.dot(s.astype(BF16), v, preferred_element_type=F32)
        num_intra = jnp.where(hid == h, r, num_intra)
        m_t_c.append(m_t)
        inter_c.append(jnp.exp(m_inter - m_t))
        rs_c.append(jnp.sum(s, axis=-1, keepdims=True))

        logw_end = b_end - b_col + i_col
        m_new = jnp.maximum(b_end + m_prev, jnp.max(logw_end, axis=0, keepdims=True))
        wend_c.append(jnp.exp(logw_end - m_new))
        mnew_c.append(m_new)
        decay_c.append(jnp.exp(b_end + m_prev - m_new))

    m_t_l = _expand_heads(m_t_c)
    inter_l = _expand_heads(inter_c)
    c_mat = c_ref[d]
    n_vec = n_ref[d]
    c_hi = c_mat.astype(BF16)
    c_lo = (c_mat - c_hi.astype(F32)).astype(BF16)
    qc = (jnp.dot(q, c_hi, preferred_element_type=F32)
          + jnp.dot(q, c_lo, preferred_element_type=F32))
    qn_l = _group_mean(qf * n_vec, bd1)
    num = num_intra + inter_l * qc
    den = _expand_heads(rs_c) + inter_l * qn_l
    h_out = num / jnp.maximum(jnp.abs(den), jnp.exp(-m_t_l))

    decay_l = _expand_heads(decay_c)
    kw = kf * _expand_heads(wend_c)
    kv = lax.dot_general(kw.astype(BF16), v, (((0,), (0,)), ((), ())),
                         preferred_element_type=F32)
    rr = lax.broadcasted_iota(jnp.int32, (MLSTM_W, MLSTM_W), 0) >> 6
    cc = lax.broadcasted_iota(jnp.int32, (MLSTM_W, MLSTM_W), 1) >> 6
    c_ref[d] = decay_l * c_mat + jnp.where(rr == cc, kv, 0.0)
    n_ref[d] = decay_l * n_vec + jnp.sum(kw, axis=0, keepdims=True)
    m_ref[d] = _expand_heads(mnew_c)
    return h_out


def _mlstm_body(qf_ref, kf_ref, vf_ref, gf_ref, gtf_ref, qb_ref, kb_ref, vb_ref, gb_ref, gtb_ref,
                bd1_ref, hf_ref, hb_ref, c_ref, n_ref, m_ref):
    @pl.when(pl.program_id(1) == 0)
    def _():
        c_ref[...] = jnp.zeros_like(c_ref)
        n_ref[...] = jnp.zeros_like(n_ref)
        m_ref[...] = jnp.zeros_like(m_ref)

    bd1 = bd1_ref[...]
    hf_ref[0] = _mlstm_direction(0, qf_ref[0], kf_ref[0], vf_ref[0], gf_ref[0], gtf_ref[0],
                                 c_ref, n_ref, m_ref, bd1)
    hb_ref[0] = _mlstm_direction(1, qb_ref[0], kb_ref[0], vb_ref[0], gb_ref[0], gtb_ref[0],
                                 c_ref, n_ref, m_ref, bd1)


def _mlstm(mq, mk, mv, gate, gate_t, bd1):
    b, s, w = mq.shape
    L = MLSTM_CHUNK
    nc = s // L
    fwd = pl.BlockSpec((1, L, w), lambda bi, c: (bi, c, 0))
    bwd = pl.BlockSpec((1, L, w), lambda bi, c: (bi, nc - 1 - c, 0))
    gfwd = pl.BlockSpec((1, L, LANES), lambda bi, c: (bi, c, 0))
    gbwd = pl.BlockSpec((1, L, LANES), lambda bi, c: (bi, nc - 1 - c, 0))
    gtfwd = pl.BlockSpec((1, N_GATES, L), lambda bi, c: (bi, 0, c))
    gtbwd = pl.BlockSpec((1, N_GATES, L), lambda bi, c: (bi, 0, nc - 1 - c))
    return pl.pallas_call(
        _mlstm_body,
        out_shape=(jax.ShapeDtypeStruct((b, s, w), F32), jax.ShapeDtypeStruct((b, s, w), F32)),
        grid=(b, nc),
        in_specs=[fwd, fwd, fwd, gfwd, gtfwd, bwd, bwd, bwd, gbwd, gtbwd, _const_spec(bd1.shape)],
        out_specs=(fwd, bwd),
        scratch_shapes=[pltpu.VMEM((2, w, w), F32), pltpu.VMEM((2, 1, w), F32), pltpu.VMEM((2, 1, w), F32)],
        compiler_params=_cparams(("parallel", "arbitrary")),
        name="mlstm_scan",
    )(mq, mk, mv, gate, gate_t, mq, mk, mv, gate, gate_t, bd1)


def _memkv_body(mem_ref, g_ref, w_ref, kg_ref, k_ref, v_ref):
    mb = _rms(mem_ref[0], g_ref[...]).astype(BF16)
    kv = jnp.dot(mb, w_ref[...], preferred_element_type=F32)
    d = kv.shape[1] // 2
    hd = d // XATTN_HEADS
    for h in range(XATTN_HEADS):
        k_ref[0, :, h * hd:(h + 1) * hd] = _rms(kv[:, h * hd:(h + 1) * hd], kg_ref[...]).astype(BF16)
    v_ref[0] = kv[:, d:].astype(BF16)


def _memkv(mem, g, wkv, k_gain):
    b, m, d = mem.shape
    spec = pl.BlockSpec((1, m, d), lambda bi: (bi, 0, 0))
    return pl.pallas_call(
        _memkv_body,
        out_shape=(jax.ShapeDtypeStruct((b, m, d), BF16), jax.ShapeDtypeStruct((b, m, d), BF16)),
        grid=(b,),
        in_specs=[spec, _const_spec((1, d)), _const_spec(wkv.shape), _const_spec(k_gain.shape)],
        out_specs=(spec, spec),
        compiler_params=_cparams(("parallel",)),
        name="memory_kv",
    )(mem, g, wkv, k_gain)


def _mix_out_body(x_ref, ya_ref, yc_ref, hf_ref, hb_ref, og_ref, mg_ref, bd_ref, wo_ref,
                  xg_ref, wq_ref, qg_ref, k_ref, v_ref, wxo_ref, o_ref):
    hs = hf_ref[0] + hb_ref[0]
    hn = hs * lax.rsqrt(_group_mean(hs * hs, bd_ref[...]) + EPS) * mg_ref[...]
    ym = (og_ref[0] * hn).astype(BF16)
    x = x_ref[0]
    x = (x + jnp.dot(ya_ref[0], wo_ref[0:ATTN_W, :], preferred_element_type=F32)
         + jnp.dot(yc_ref[0], wo_ref[ATTN_W:ATTN_W + CONV_CH, :], preferred_element_type=F32)
         + jnp.dot(ym, wo_ref[ATTN_W + CONV_CH:, :], preferred_element_type=F32))

    xb = _rms(x, xg_ref[...]).astype(BF16)
    qx = jnp.dot(xb, wq_ref[...], preferred_element_type=F32)
    hd = qx.shape[1] // XATTN_HEADS
    outs = []
    for h in range(XATTN_HEADS):
        sl = slice(h * hd, (h + 1) * hd)
        qh = (_rms(qx[:, sl], qg_ref[...]) * (hd ** -0.5)).astype(BF16)
        s = lax.dot_general(qh, k_ref[0, :, sl], (((1,), (1,)), ((), ())), preferred_element_type=F32)
        p = jnp.exp(s - jnp.max(s, axis=-1, keepdims=True))
        l = jnp.sum(p, axis=-1, keepdims=True)
        outs.append((jnp.dot(p.astype(BF16), v_ref[0, :, sl], preferred_element_type=F32) / l).astype(BF16))
    o = jnp.concatenate(outs, axis=1)
    o_ref[0] = x + jnp.dot(o, wxo_ref[...], preferred_element_type=F32)


def _mix_out(x, ya, yc, hf, hb, og, m_gain, bd, w_out, x_gain, wq, q_gain, mk, mv, wo, *, tm):
    b, s, d = x.shape
    tok = lambda width: pl.BlockSpec((1, tm, width), lambda bi, i: (bi, i, 0))
    mem_spec = pl.BlockSpec((1,) + mk.shape[1:], lambda bi, i: (bi, 0, 0))
    return pl.pallas_call(
        _mix_out_body,
        out_shape=jax.ShapeDtypeStruct((b, s, d), F32),
        grid=(b, s // tm),
        in_specs=[tok(d), tok(ATTN_W), tok(CONV_CH), tok(MLSTM_W), tok(MLSTM_W), tok(MLSTM_W),
                  _const_spec(m_gain.shape), _const_spec(bd.shape), _const_spec(w_out.shape),
                  _const_spec(x_gain.shape), _const_spec(wq.shape), _const_spec(q_gain.shape),
                  mem_spec, mem_spec, _const_spec(wo.shape)],
        out_specs=tok(d),
        compiler_params=_cparams(("parallel", "parallel")),
        name="mix_out_cross_attention",
    )(x, ya, yc, hf, hb, og, m_gain, bd, w_out, x_gain, wq, q_gain, mk, mv, wo)


def _rope_tables(s_len):
    rows = s_len // GRID_W
    row_idx = jnp.repeat(jnp.arange(rows, dtype=jnp.int32), GRID_W).astype(F32)
    col_idx = jnp.tile(jnp.arange(GRID_W, dtype=jnp.int32), rows).astype(F32)
    n_freq = HEAD_DIM // 4
    inv_freq = jnp.float32(ROPE_THETA) ** (-jnp.arange(n_freq, dtype=F32) / n_freq)
    ang = jnp.concatenate([row_idx[:, None] * inv_freq, col_idx[:, None] * inv_freq], axis=-1)
    cos = jnp.repeat(jnp.cos(ang), 2, axis=-1)
    sin = jnp.repeat(jnp.sin(ang), 2, axis=-1) * jnp.tile(jnp.array([-1.0, 1.0], F32), HEAD_DIM // 2)
    return jnp.tile(cos, (1, 2)), jnp.tile(sin, (1, 2))


def _block_diag(scale):
    idx = jnp.arange(MLSTM_W) // MLSTM_HEAD_DIM
    return jnp.where(idx[:, None] == idx[None, :], scale, 0.0).astype(BF16)


def _pick_tile(n, pref):
    t = min(n, pref)
    while n % t:
        t //= 2
    return t


def kernel(x, mem, ffn1_norm, ffn1_w13, ffn1_w2, mix_norm, w_in, attn_q_norm, attn_k_norm, conv_dw_w,
           conv_dw_b, conv_ln_g, conv_ln_b, mlstm_gate_b, mlstm_out_norm, w_out, xattn_norm, mem_norm,
           xattn_wq, xattn_wkv, xattn_q_norm, xattn_k_norm, xattn_wo, ffn2_norm, ffn2_w13, ffn2_w2):
    b, s, d = x.shape
    depth = w_in.shape[0]
    cos, sin = _rope_tables(s)
    bd_mean = _block_diag(1.0 / MLSTM_HEAD_DIM)
    bd_sum = _block_diag(1.0)

    tm_ffn = _pick_tile(b * s, 512)
    tm_tok = _pick_tile(s, 512)
    tq = _pick_tile(s, 256)
    tk = _pick_tile(s, 512)
    ts = _pick_tile(s, 256)
    row = lambda v: v.reshape(1, -1)

    for l in range(depth):
        x = _ffn(x.reshape(b * s, d), row(ffn1_norm[l]), ffn1_w13[l].astype(BF16), ffn1_w2[l].astype(BF16),
                 tm=tm_ffn, n_chunks=2).reshape(b, s, d)

        w_l = w_in[l]
        gate_b = mlstm_gate_b[l].reshape(1, N_GATES)
        lane_pad = ((0, 0), (0, LANES - N_GATES))
        q, kp, vp, u, mq, mk, mv, og, gate, gate_t = _inproj(
            x, row(mix_norm[l]), w_l[:, :OFF_MG].astype(BF16), jnp.pad(w_l[:, OFF_MG:], lane_pad).astype(BF16),
            w_l[:, OFF_MG:].T.astype(BF16), jnp.pad(gate_b, lane_pad), gate_b.reshape(N_GATES, 1),
            row(jnp.tile(attn_q_norm[l], ATTN_HEADS)), row(jnp.tile(attn_k_norm[l], ATTN_KV_HEADS)),
            cos, sin, bd_mean, tm=tm_tok)

        ya = _attention(q, kp, vp, tq=tq, tk=tk)
        yc = _conv(u, conv_dw_w[l], row(conv_dw_b[l]), row(conv_ln_g[l]), row(conv_ln_b[l]), ts=ts)
        hf, hb = _mlstm(mq, mk, mv, gate, gate_t, bd_sum)
        mem_k, mem_v = _memkv(mem, row(mem_norm[l]), xattn_wkv[l].astype(BF16), row(xattn_k_norm[l]))
        x = _mix_out(x, ya, yc, hf, hb, og, row(mlstm_out_norm[l]), bd_mean, w_out[l].astype(BF16),
                     row(xattn_norm[l]), xattn_wq[l].astype(BF16), row(xattn_q_norm[l]), mem_k, mem_v,
                     xattn_wo[l].astype(BF16), tm=tm_tok)

        x = _ffn(x.reshape(b * s, d), row(ffn2_norm[l]), ffn2_w13[l].astype(BF16), ffn2_w2[l].astype(BF16),
                 tm=tm_ffn, n_chunks=2).reshape(b, s, d)
    return x
```

```python
import functools

import jax
import jax.numpy as jnp
from jax import lax
from jax.experimental import pallas as pl
from jax.experimental.pallas import tpu as pltpu

F32 = jnp.float32
BF16 = jnp.bfloat16

GRID_W = 64
EPS = 1e-6
LOG2E = 1.4426950408889634
ATTN_HEADS = 8
ATTN_KV_HEADS = 2
HEAD_DIM = 64
ATTN_W = ATTN_HEADS * HEAD_DIM
KV_W = ATTN_KV_HEADS * HEAD_DIM
ROPE_THETA = 10000.0
CONV_CH = 256
CONV_WIDTH = 31
CONV_PAD = CONV_WIDTH // 2
MLSTM_HEADS = 4
MLSTM_HEAD_DIM = 64
MLSTM_W = MLSTM_HEADS * MLSTM_HEAD_DIM
MLSTM_CHUNK = 128
N_GATES = 16
XATTN_HEADS = 4
OFF_K = ATTN_W
OFF_V = OFF_K + KV_W
OFF_CONV = OFF_V + KV_W
OFF_MQ = OFF_CONV + 2 * CONV_CH
OFF_MK = OFF_MQ + MLSTM_W
OFF_MV = OFF_MK + MLSTM_W
OFF_MO = OFF_MV + MLSTM_W
OFF_MG = OFF_MO + MLSTM_W

LANES = 128
HALO_ROWS = 16
KEY_CHUNK = 512
VMEM_LIMIT = 56 * 1024 * 1024


def _cparams(sem):
    return pltpu.CompilerParams(dimension_semantics=sem, vmem_limit_bytes=VMEM_LIMIT)


def _const_spec(shape):
    nd = len(shape)
    return pl.BlockSpec(shape, lambda *_: (0,) * nd)


def _rms(x, g):
    return x * lax.rsqrt(jnp.mean(x * x, axis=-1, keepdims=True) + EPS) * g


def _group_mean(sq, bd):
    hi = sq.astype(BF16)
    lo = (sq - hi.astype(F32)).astype(BF16)
    return (jnp.dot(hi, bd, preferred_element_type=F32)
            + jnp.dot(lo, bd, preferred_element_type=F32))


def _swap_pairs(x):
    nxt = pltpu.roll(x, LANES - 1, axis=1)
    prv = pltpu.roll(x, 1, axis=1)
    lane = lax.broadcasted_iota(jnp.int32, x.shape, 1)
    return jnp.where((lane & 1) == 0, nxt, prv)


def _sigmoid(x):
    return 1.0 / (1.0 + jnp.exp(-x))


def _ffn_body(x_ref, g_ref, w13_ref, w2_ref, o_ref, *, d_ff, n_chunks):
    x = x_ref[...]
    xb = _rms(x, g_ref[...]).astype(BF16)
    fc = d_ff // n_chunks
    acc = None
    for c in range(n_chunks):
        a = jnp.dot(xb, w13_ref[:, c * fc:(c + 1) * fc], preferred_element_type=F32)
        b = jnp.dot(xb, w13_ref[:, d_ff + c * fc:d_ff + (c + 1) * fc], preferred_element_type=F32)
        act = (a * _sigmoid(a) * b).astype(BF16)
        part = jnp.dot(act, w2_ref[c * fc:(c + 1) * fc, :], preferred_element_type=F32)
        acc = part if acc is None else acc + part
    o_ref[...] = x + 0.5 * acc


def _ffn(x2d, g, w13, w2, *, tm, n_chunks):
    n, d = x2d.shape
    d_ff = w2.shape[0]
    return pl.pallas_call(
        functools.partial(_ffn_body, d_ff=d_ff, n_chunks=n_chunks),
        out_shape=jax.ShapeDtypeStruct((n, d), F32),
        grid=(n // tm,),
        in_specs=[
            pl.BlockSpec((tm, d), lambda i: (i, 0)),
            _const_spec((1, d)),
            pl.BlockSpec((d, 2 * d_ff), lambda i: (0, 0), pipeline_mode=pl.Buffered(1)),
            pl.BlockSpec((d_ff, d), lambda i: (0, 0), pipeline_mode=pl.Buffered(1)),
        ],
        out_specs=pl.BlockSpec((tm, d), lambda i: (i, 0)),
        compiler_params=_cparams(("parallel",)),
        name="swiglu_half_step",
    )(x2d, g, w13, w2)


def _inproj_body(x_ref, g_ref, w_ref, wg_ref, wgt_ref, gb_ref, gbt_ref, qg_ref, kg_ref,
                 cos_ref, sin_ref, bd_ref,
                 q_ref, k_ref, vt_ref, u_ref, mq_ref, mk_ref, mv_ref, og_ref, gate_ref, gatet_ref):
    xb = _rms(x_ref[0], g_ref[...]).astype(BF16)
    h = jnp.dot(xb, w_ref[...], preferred_element_type=F32)
    bd = bd_ref[...]
    cos = cos_ref[...]
    sin = sin_ref[...]

    hq = h[:, 0:ATTN_W]
    ms = jnp.concatenate([_group_mean(hq[:, 0:256] * hq[:, 0:256], bd),
                          _group_mean(hq[:, 256:512] * hq[:, 256:512], bd)], axis=1)
    qn = hq * lax.rsqrt(ms + EPS) * qg_ref[...]
    in_first = lax.broadcasted_iota(jnp.int32, cos.shape, 1) < HEAD_DIM
    group = ATTN_HEADS // ATTN_KV_HEADS
    for j in range(ATTN_W // LANES):
        z = qn[:, j * LANES:(j + 1) * LANES]
        r = (z * cos + _swap_pairs(z) * sin) * (HEAD_DIM ** -0.5 * LOG2E)
        rolled = pltpu.roll(r, HEAD_DIM, axis=1)
        for half in range(2):
            hd = 2 * j + half
            g = hd // group
            z = r if half == g else rolled
            keep = jnp.where(in_first, z, 0.0) if g == 0 else jnp.where(in_first, 0.0, z)
            q_ref[0, hd] = keep.astype(BF16)

    hk = h[:, OFF_K:OFF_V]
    msk = _group_mean(hk * hk, bd_ref[0:LANES, 0:LANES])
    kn = hk * lax.rsqrt(msk + EPS) * kg_ref[...]
    k_ref[0] = (kn * cos + _swap_pairs(kn) * sin).astype(BF16)
    hv = h[:, OFF_V:OFF_CONV]
    for r in range(hv.shape[0] // KEY_CHUNK):
        vt = hv[r * KEY_CHUNK:(r + 1) * KEY_CHUNK, :].T.astype(BF16)
        for g in range(ATTN_KV_HEADS):
            vt_ref[0, r, g] = vt[g * HEAD_DIM:(g + 1) * HEAD_DIM, :]

    u_ref[0] = h[:, OFF_CONV:OFF_CONV + CONV_CH] * _sigmoid(h[:, OFF_CONV + CONV_CH:OFF_MQ])

    mq_ref[0] = h[:, OFF_MQ:OFF_MK].astype(BF16)
    mk_ref[0] = (h[:, OFF_MK:OFF_MV] * (MLSTM_HEAD_DIM ** -0.5)).astype(BF16)
    mv_ref[0] = h[:, OFF_MV:OFF_MO].astype(BF16)
    og_ref[0] = _sigmoid(h[:, OFF_MO:OFF_MG])

    def log_gate(pre, is_forget):
        ls = jnp.minimum(pre, 0.0) - jnp.log1p(jnp.exp(-jnp.abs(pre)))
        return jnp.where(is_forget, ls, pre)

    gp = jnp.dot(xb, wg_ref[...], preferred_element_type=F32) + gb_ref[...]
    col = lax.broadcasted_iota(jnp.int32, gp.shape, 1)
    gate_ref[0] = log_gate(gp, ((col >> 2) & 1) == 1)
    gpt = lax.dot_general(wgt_ref[...], xb, (((1,), (1,)), ((), ())),
                          preferred_element_type=F32) + gbt_ref[...]
    row = lax.broadcasted_iota(jnp.int32, gpt.shape, 0)
    gatet_ref[0] = log_gate(gpt, ((row >> 2) & 1) == 1)


def _inproj(x, g, w_main, w_gate, w_gate_t, gate_b, gate_bt, q_gain, k_gain, cos, sin, bd, *, tm):
    b, s, d = x.shape
    tok = lambda width, dt: jax.ShapeDtypeStruct((b, s, width), dt)
    tok_spec = lambda width: pl.BlockSpec((1, tm, width), lambda bi, i: (bi, i, 0))
    q_spec = pl.BlockSpec((1, ATTN_HEADS, tm, LANES), lambda bi, i: (bi, 0, i, 0))
    vt_shape = (b, s // KEY_CHUNK, ATTN_KV_HEADS, HEAD_DIM, KEY_CHUNK)
    vt_spec = pl.BlockSpec((1, tm // KEY_CHUNK) + vt_shape[2:], lambda bi, i: (bi, i, 0, 0, 0))
    return pl.pallas_call(
        _inproj_body,
        out_shape=(
            jax.ShapeDtypeStruct((b, ATTN_HEADS, s, LANES), BF16),
            tok(KV_W, BF16),
            jax.ShapeDtypeStruct(vt_shape, BF16),
            tok(CONV_CH, F32),
            tok(MLSTM_W, BF16), tok(MLSTM_W, BF16), tok(MLSTM_W, BF16),
            tok(MLSTM_W, F32),
            tok(LANES, F32),
            jax.ShapeDtypeStruct((b, N_GATES, s), F32),
        ),
        grid=(b, s // tm),
        in_specs=[
            tok_spec(d),
            _const_spec((1, d)),
            _const_spec(w_main.shape),
            _const_spec(w_gate.shape),
            _const_spec(w_gate_t.shape),
            _const_spec((1, LANES)),
            _const_spec((N_GATES, 1)),
            _const_spec((1, ATTN_W)),
            _const_spec((1, LANES)),
            pl.BlockSpec((tm, LANES), lambda bi, i: (i, 0)),
            pl.BlockSpec((tm, LANES), lambda bi, i: (i, 0)),
            _const_spec(bd.shape),
        ],
        out_specs=(
            q_spec, tok_spec(KV_W), vt_spec, tok_spec(CONV_CH),
            tok_spec(MLSTM_W), tok_spec(MLSTM_W), tok_spec(MLSTM_W), tok_spec(MLSTM_W),
            tok_spec(LANES),
            pl.BlockSpec((1, N_GATES, tm), lambda bi, i: (bi, 0, i)),
        ),
        compiler_params=_cparams(("parallel", "parallel")),
        name="input_projection",
    )(x, g, w_main, w_gate, w_gate_t, gate_b, gate_bt, q_gain, k_gain, cos, sin, bd)


def _attn_body(q_ref, k_ref, vt_ref, o_ref, s_ref):
    tq = q_ref.shape[2]
    n_chunks = vt_ref.shape[1]
    tk = vt_ref.shape[4]
    nt = (((1,), (1,)), ((), ()))

    def scores(c, slot):
        kc = k_ref[0, pl.ds(pl.multiple_of(c * tk, tk), tk), :]
        for j in range(2):
            s_ref[slot, j] = lax.dot_general(kc, q_ref[0, j], nt, preferred_element_type=F32)

    def consume(c, slot, state):
        vtc = vt_ref[0, c, 0]
        out = []
        for j in range(2):
            m, l, acc = state[j]
            s = s_ref[slot, j]
            m_new = jnp.maximum(m, jnp.max(s, axis=0, keepdims=True))
            alpha = jnp.exp2(m - m_new)
            p = jnp.exp2(s - m_new)
            l_new = alpha * l + jnp.sum(p, axis=0, keepdims=True)
            acc_new = alpha * acc + jnp.dot(vtc, p.astype(BF16), preferred_element_type=F32)
            out.append((m_new, l_new, acc_new))
        return tuple(out)

    def step(i, state):
        scores(2 * i + 1, 1)
        state = consume(2 * i, 0, state)
        scores(jnp.minimum(2 * i + 2, n_chunks - 1), 0)
        return consume(2 * i + 1, 1, state)

    init = tuple((jnp.full((1, tq), -jnp.inf, F32), jnp.zeros((1, tq), F32),
                  jnp.zeros((HEAD_DIM, tq), F32)) for _ in range(2))
    scores(0, 0)
    (_, l0, a0), (_, l1, a1) = lax.fori_loop(0, n_chunks // 2, step, init)
    o_ref[0] = jnp.concatenate([a0 / l0, a1 / l1], axis=0).T.astype(BF16)


def _attention(q, k, vt, *, tq):
    b, _, s, _ = q.shape
    n_chunks, _, hd, tk = vt.shape[1:]
    assert n_chunks % 2 == 0
    heads_per_kv = ATTN_HEADS // ATTN_KV_HEADS
    return pl.pallas_call(
        _attn_body,
        out_shape=jax.ShapeDtypeStruct((b, s, ATTN_W), BF16),
        grid=(b, ATTN_HEADS // 2, s // tq),
        in_specs=[pl.BlockSpec((1, 2, tq, LANES), lambda bi, p, i: (bi, p, i, 0)),
                  pl.BlockSpec((1, s, KV_W), lambda bi, p, i: (bi, 0, 0)),
                  pl.BlockSpec((1, n_chunks, 1, hd, tk), lambda bi, p, i: (bi, 0, (2 * p) // heads_per_kv, 0, 0))],
        out_specs=pl.BlockSpec((1, tq, LANES), lambda bi, p, i: (bi, i, p)),
        scratch_shapes=[pltpu.VMEM((2, 2, tk, tq), F32)],
        compiler_params=_cparams(("parallel", "parallel", "arbitrary")),
        name="gqa_attention",
    )(q, k, vt)


def _conv_body(prev_ref, cur_ref, next_ref, w_ref, b_ref, lg_ref, lb_ref, o_ref):
    i = pl.program_id(1)
    ts = cur_ref.shape[1]
    prev = jnp.where(i > 0, prev_ref[0], 0.0)
    nxt = jnp.where(i < pl.num_programs(1) - 1, next_ref[0], 0.0)
    win = jnp.concatenate([prev, cur_ref[0], nxt], axis=0)
    w = w_ref[...]
    acc = jnp.zeros((ts, CONV_CH), F32) + b_ref[...]
    for t in range(CONV_WIDTH):
        off = HALO_ROWS - CONV_PAD + t
        acc = acc + win[off:off + ts, :] * w[t:t + 1, :]
    mu = jnp.mean(acc, axis=-1, keepdims=True)
    cen = acc - mu
    var = jnp.mean(cen * cen, axis=-1, keepdims=True)
    un = cen * lax.rsqrt(var + EPS) * lg_ref[...] + lb_ref[...]
    o_ref[0] = (un * _sigmoid(un)).astype(BF16)


def _conv(u, dw_w, dw_b, ln_g, ln_b, *, ts):
    b, s, c = u.shape
    r = ts // HALO_ROWS
    n_halo = s // HALO_ROWS
    return pl.pallas_call(
        _conv_body,
        out_shape=jax.ShapeDtypeStruct((b, s, c), BF16),
        grid=(b, s // ts),
        in_specs=[
            pl.BlockSpec((1, HALO_ROWS, c), lambda bi, i: (bi, jnp.maximum(i * r - 1, 0), 0)),
            pl.BlockSpec((1, ts, c), lambda bi, i: (bi, i, 0)),
            pl.BlockSpec((1, HALO_ROWS, c), lambda bi, i: (bi, jnp.minimum((i + 1) * r, n_halo - 1), 0)),
            _const_spec((CONV_WIDTH, c)),
            _const_spec((1, c)), _const_spec((1, c)), _const_spec((1, c)),
        ],
        out_specs=pl.BlockSpec((1, ts, c), lambda bi, i: (bi, i, 0)),
        compiler_params=_cparams(("parallel", "parallel")),
        name="conformer_conv",
    )(u, u, u, dw_w, dw_b, ln_g, ln_b)


def _expand_heads(cols):
    rows = cols[0].shape[0]
    hid = lax.broadcasted_iota(jnp.int32, (rows, MLSTM_W), 1) >> 6
    out = jnp.broadcast_to(cols[3], (rows, MLSTM_W))
    for h in (2, 1, 0):
        out = jnp.where(hid == h, cols[h], out)
    return out


def _split3(x):
    a = x.astype(BF16)
    r = x - a.astype(F32)
    b = r.astype(BF16)
    c = (r - b.astype(F32)).astype(BF16)
    return a, b, c


def _mlstm_direction(d, q, k, v, gate, gate_t, c_ref, n_ref, m_ref, bd1):
    L = q.shape[0]
    r_i = lax.broadcasted_iota(jnp.int32, (L, L), 0)
    c_i = lax.broadcasted_iota(jnp.int32, (L, L), 1)
    keep = (c_i <= r_i) if d == 0 else (c_i >= r_i)
    tri = jnp.where(keep, 1.0, 0.0).astype(BF16)
    tri_t = jnp.where((r_i <= c_i) if d == 0 else (r_i >= c_i), 1.0, 0.0).astype(BF16)

    cum_col = sum(jnp.dot(tri, t, preferred_element_type=F32) for t in _split3(gate))
    cum_row = sum(jnp.dot(t, tri_t, preferred_element_type=F32) for t in _split3(gate_t))
    tot = jnp.sum(gate, axis=0, keepdims=True)

    hid = lax.broadcasted_iota(jnp.int32, (L, MLSTM_W), 1) >> 6
    m_prev_l = m_ref[d]
    qf = q.astype(F32)
    kf = k.astype(F32)

    m_t_c, inter_c, rs_c, wend_c, mnew_c, decay_c = [], [], [], [], [], []
    num_intra = jnp.zeros((L, MLSTM_W), F32)
    for h in range(MLSTM_HEADS):
        ji = d * 8 + h
        jf = d * 8 + MLSTM_HEADS + h
        m_prev = m_prev_l[:, h * MLSTM_HEAD_DIM:h * MLSTM_HEAD_DIM + 1]
        b_col = cum_col[:, jf:jf + 1]
        i_col = gate[:, ji:ji + 1]
        b_row = cum_row[jf:jf + 1, :]
        i_row = gate_t[ji:ji + 1, :]
        b_end = tot[:, jf:jf + 1]

        logw = jnp.where(keep, b_col - b_row + i_row, -jnp.inf)
        m_inter = b_col + m_prev
        m_t = jnp.maximum(jnp.max(logw, axis=-1, keepdims=True), m_inter)
        w = jnp.exp(logw - m_t)
        qm = jnp.where(hid == h, qf, 0.0).astype(BF16)
        s = lax.dot_general(qm, k, (((1,), (1,)), ((), ())), preferred_element_type=F32) * w
        r = jnp.dot(s.astype(BF16), v, preferred_element_type=F32)
        num_intra = jnp.where(hid == h, r, num_intra)
        m_t_c.append(m_t)
        inter_c.append(jnp.exp(m_inter - m_t))
        rs_c.append(jnp.sum(s, axis=-1, keepdims=True))

        logw_end = b_end - b_col + i_col
        m_new = jnp.maximum(b_end + m_prev, jnp.max(logw_end, axis=0, keepdims=True))
        wend_c.append(jnp.exp(logw_end - m_new))
        mnew_c.append(m_new)
        decay_c.append(jnp.exp(b_end + m_prev - m_new))

    m_t_l = _expand_heads(m_t_c)
    inter_l = _expand_heads(inter_c)
    c_mat = c_ref[d]
    n_vec = n_ref[d]
    c_hi = c_mat.astype(BF16)
    c_lo = (c_mat - c_hi.astype(F32)).astype(BF16)
    qc = (jnp.dot(q, c_hi, preferred_element_type=F32)
          + jnp.dot(q, c_lo, preferred_element_type=F32))
    qn_l = _group_mean(qf * n_vec, bd1)
    num = num_intra + inter_l * qc
    den = _expand_heads(rs_c) + inter_l * qn_l
    h_out = num / jnp.maximum(jnp.abs(den), jnp.exp(-m_t_l))

    decay_l = _expand_heads(decay_c)
    kw = kf * _expand_heads(wend_c)
    kv = lax.dot_general(kw.astype(BF16), v, (((0,), (0,)), ((), ())),
                         preferred_element_type=F32)
    rr = lax.broadcasted_iota(jnp.int32, (MLSTM_W, MLSTM_W), 0) >> 6
    cc = lax.broadcasted_iota(jnp.int32, (MLSTM_W, MLSTM_W), 1) >> 6
    c_ref[d] = decay_l * c_mat + jnp.where(rr == cc, kv, 0.0)
    n_ref[d] = decay_l * n_vec + jnp.sum(kw, axis=0, keepdims=True)
    m_ref[d] = _expand_heads(mnew_c)
    return h_out


def _mlstm_body(qf_ref, kf_ref, vf_ref, gf_ref, gtf_ref, qb_ref, kb_ref, vb_ref, gb_ref, gtb_ref,
                bd1_ref, hf_ref, hb_ref, c_ref, n_ref, m_ref):
    @pl.when(pl.program_id(1) == 0)
    def _():
        c_ref[...] = jnp.zeros_like(c_ref)
        n_ref[...] = jnp.zeros_like(n_ref)
        m_ref[...] = jnp.zeros_like(m_ref)

    bd1 = bd1_ref[...]
    hf_ref[0] = _mlstm_direction(0, qf_ref[0], kf_ref[0], vf_ref[0], gf_ref[0], gtf_ref[0],
                                 c_ref, n_ref, m_ref, bd1)
    hb_ref[0] = _mlstm_direction(1, qb_ref[0], kb_ref[0], vb_ref[0], gb_ref[0], gtb_ref[0],
                                 c_ref, n_ref, m_ref, bd1)


def _mlstm(mq, mk, mv, gate, gate_t, bd1):
    b, s, w = mq.shape
    L = MLSTM_CHUNK
    nc = s // L
    fwd = pl.BlockSpec((1, L, w), lambda bi, c: (bi, c, 0))
    bwd = pl.BlockSpec((1, L, w), lambda bi, c: (bi, nc - 1 - c, 0))
    gfwd = pl.BlockSpec((1, L, LANES), lambda bi, c: (bi, c, 0))
    gbwd = pl.BlockSpec((1, L, LANES), lambda bi, c: (bi, nc - 1 - c, 0))
    gtfwd = pl.BlockSpec((1, N_GATES, L), lambda bi, c: (bi, 0, c))
    gtbwd = pl.BlockSpec((1, N_GATES, L), lambda bi, c: (bi, 0, nc - 1 - c))
    return pl.pallas_call(
        _mlstm_body,
        out_shape=(jax.ShapeDtypeStruct((b, s, w), F32), jax.ShapeDtypeStruct((b, s, w), F32)),
        grid=(b, nc),
        in_specs=[fwd, fwd, fwd, gfwd, gtfwd, bwd, bwd, bwd, gbwd, gtbwd, _const_spec(bd1.shape)],
        out_specs=(fwd, bwd),
        scratch_shapes=[pltpu.VMEM((2, w, w), F32), pltpu.VMEM((2, 1, w), F32), pltpu.VMEM((2, 1, w), F32)],
        compiler_params=_cparams(("parallel", "arbitrary")),
        name="mlstm_scan",
    )(mq, mk, mv, gate, gate_t, mq, mk, mv, gate, gate_t, bd1)


def _memkv_body(mem_ref, g_ref, w_ref, kg_ref, k_ref, v_ref):
    mb = _rms(mem_ref[0], g_ref[...]).astype(BF16)
    kv = jnp.dot(mb, w_ref[...], preferred_element_type=F32)
    d = kv.shape[1] // 2
    hd = d // XATTN_HEADS
    for h in range(XATTN_HEADS):
        k_ref[0, :, h * hd:(h + 1) * hd] = _rms(kv[:, h * hd:(h + 1) * hd], kg_ref[...]).astype(BF16)
    v_ref[0] = kv[:, d:].astype(BF16)


def _memkv(mem, g, wkv, k_gain):
    b, m, d = mem.shape
    spec = pl.BlockSpec((1, m, d), lambda bi: (bi, 0, 0))
    return pl.pallas_call(
        _memkv_body,
        out_shape=(jax.ShapeDtypeStruct((b, m, d), BF16), jax.ShapeDtypeStruct((b, m, d), BF16)),
        grid=(b,),
        in_specs=[spec, _const_spec((1, d)), _const_spec(wkv.shape), _const_spec(k_gain.shape)],
        out_specs=(spec, spec),
        compiler_params=_cparams(("parallel",)),
        name="memory_kv",
    )(mem, g, wkv, k_gain)


def _mix_out_body(x_ref, ya_ref, yc_ref, hf_ref, hb_ref, og_ref, mg_ref, bd_ref, wo_ref,
                  xg_ref, wq_ref, qg_ref, k_ref, v_ref, wxo_ref, o_ref):
    hs = hf_ref[0] + hb_ref[0]
    hn = hs * lax.rsqrt(_group_mean(hs * hs, bd_ref[...]) + EPS) * mg_ref[...]
    ym = (og_ref[0] * hn).astype(BF16)
    x = x_ref[0]
    x = (x + jnp.dot(ya_ref[0], wo_ref[0:ATTN_W, :], preferred_element_type=F32)
         + jnp.dot(yc_ref[0], wo_ref[ATTN_W:ATTN_W + CONV_CH, :], preferred_element_type=F32)
         + jnp.dot(ym, wo_ref[ATTN_W + CONV_CH:, :], preferred_element_type=F32))

    xb = _rms(x, xg_ref[...]).astype(BF16)
    qx = jnp.dot(xb, wq_ref[...], preferred_element_type=F32)
    hd = qx.shape[1] // XATTN_HEADS
    outs = []
    for h in range(XATTN_HEADS):
        sl = slice(h * hd, (h + 1) * hd)
        qh = (_rms(qx[:, sl], qg_ref[...]) * (hd ** -0.5)).astype(BF16)
        s = lax.dot_general(qh, k_ref[0, :, sl], (((1,), (1,)), ((), ())), preferred_element_type=F32)
        p = jnp.exp(s - jnp.max(s, axis=-1, keepdims=True))
        l = jnp.sum(p, axis=-1, keepdims=True)
        outs.append((jnp.dot(p.astype(BF16), v_ref[0, :, sl], preferred_element_type=F32) / l).astype(BF16))
    o = jnp.concatenate(outs, axis=1)
    o_ref[0] = x + jnp.dot(o, wxo_ref[...], preferred_element_type=F32)


def _mix_out(x, ya, yc, hf, hb, og, m_gain, bd, w_out, x_gain, wq, q_gain, mk, mv, wo, *, tm):
    b, s, d = x.shape
    tok = lambda width: pl.BlockSpec((1, tm, width), lambda bi, i: (bi, i, 0))
    mem_spec = pl.BlockSpec((1,) + mk.shape[1:], lambda bi, i: (bi, 0, 0))
    return pl.pallas_call(
        _mix_out_body,
        out_shape=jax.ShapeDtypeStruct((b, s, d), F32),
        grid=(b, s // tm),
        in_specs=[tok(d), tok(ATTN_W), tok(CONV_CH), tok(MLSTM_W), tok(MLSTM_W), tok(MLSTM_W),
                  _const_spec(m_gain.shape), _const_spec(bd.shape), _const_spec(w_out.shape),
                  _const_spec(x_gain.shape), _const_spec(wq.shape), _const_spec(q_gain.shape),
                  mem_spec, mem_spec, _const_spec(wo.shape)],
        out_specs=tok(d),
        compiler_params=_cparams(("parallel", "parallel")),
        name="mix_out_cross_attention",
    )(x, ya, yc, hf, hb, og, m_gain, bd, w_out, x_gain, wq, q_gain, mk, mv, wo)


def _rope_tables(s_len):
    rows = s_len // GRID_W
    row_idx = jnp.repeat(jnp.arange(rows, dtype=jnp.int32), GRID_W).astype(F32)
    col_idx = jnp.tile(jnp.arange(GRID_W, dtype=jnp.int32), rows).astype(F32)
    n_freq = HEAD_DIM // 4
    inv_freq = jnp.float32(ROPE_THETA) ** (-jnp.arange(n_freq, dtype=F32) / n_freq)
    ang = jnp.concatenate([row_idx[:, None] * inv_freq, col_idx[:, None] * inv_freq], axis=-1)
    cos = jnp.repeat(jnp.cos(ang), 2, axis=-1)
    sin = jnp.repeat(jnp.sin(ang), 2, axis=-1) * jnp.tile(jnp.array([-1.0, 1.0], F32), HEAD_DIM // 2)
    return jnp.tile(cos, (1, 2)), jnp.tile(sin, (1, 2))


def _block_diag(scale):
    idx = jnp.arange(MLSTM_W) // MLSTM_HEAD_DIM
    return jnp.where(idx[:, None] == idx[None, :], scale, 0.0).astype(BF16)


def _pick_tile(n, pref):
    t = min(n, pref)
    while n % t:
        t //= 2
    return t


def kernel(x, mem, ffn1_norm, ffn1_w13, ffn1_w2, mix_norm, w_in, attn_q_norm, attn_k_norm, conv_dw_w,
           conv_dw_b, conv_ln_g, conv_ln_b, mlstm_gate_b, mlstm_out_norm, w_out, xattn_norm, mem_norm,
           xattn_wq, xattn_wkv, xattn_q_norm, xattn_k_norm, xattn_wo, ffn2_norm, ffn2_w13, ffn2_w2):
    b, s, d = x.shape
    depth = w_in.shape[0]
    cos, sin = _rope_tables(s)
    bd_mean = _block_diag(1.0 / MLSTM_HEAD_DIM)
    bd_sum = _block_diag(1.0)

    tm_ffn = _pick_tile(b * s, 512)
    tm_tok = _pick_tile(s, 512)
    tq = _pick_tile(s, 256)
    ts = _pick_tile(s, 256)
    row = lambda v: v.reshape(1, -1)

    for l in range(depth):
        x = _ffn(x.reshape(b * s, d), row(ffn1_norm[l]), ffn1_w13[l].astype(BF16), ffn1_w2[l].astype(BF16),
                 tm=tm_ffn, n_chunks=2).reshape(b, s, d)

        w_l = w_in[l]
        gate_b = mlstm_gate_b[l].reshape(1, N_GATES)
        lane_pad = ((0, 0), (0, LANES - N_GATES))
        q, kk, vt, u, mq, mk, mv, og, gate, gate_t = _inproj(
            x, row(mix_norm[l]), w_l[:, :OFF_MG].astype(BF16), jnp.pad(w_l[:, OFF_MG:], lane_pad).astype(BF16),
            w_l[:, OFF_MG:].T.astype(BF16), jnp.pad(gate_b, lane_pad), gate_b.reshape(N_GATES, 1),
            row(jnp.tile(attn_q_norm[l], ATTN_HEADS)), row(jnp.tile(attn_k_norm[l], ATTN_KV_HEADS)),
            cos, sin, bd_mean, tm=tm_tok)

        ya = _attention(q, kk, vt, tq=tq)
        yc = _conv(u, conv_dw_w[l], row(conv_dw_b[l]), row(conv_ln_g[l]), row(conv_ln_b[l]), ts=ts)
        hf, hb = _mlstm(mq, mk, mv, gate, gate_t, bd_sum)
        mem_k, mem_v = _memkv(mem, row(mem_norm[l]), xattn_wkv[l].astype(BF16), row(xattn_k_norm[l]))
        x = _mix_out(x, ya, yc, hf, hb, og, row(mlstm_out_norm[l]), bd_mean, w_out[l].astype(BF16),
                     row(xattn_norm[l]), xattn_wq[l].astype(BF16), row(xattn_q_norm[l]), mem_k, mem_v,
                     xattn_wo[l].astype(BF16), tm=tm_tok)

        x = _ffn(x.reshape(b * s, d), row(ffn2_norm[l]), ffn2_w13[l].astype(BF16), ffn2_w2[l].astype(BF16),
                 tm=tm_ffn, n_chunks=2).reshape(b, s, d)
    return x
```

```python
import functools

import jax
import jax.numpy as jnp
from jax import lax
from jax.experimental import pallas as pl
from jax.experimental.pallas import tpu as pltpu

F32 = jnp.float32
BF16 = jnp.bfloat16

GRID_W = 64
EPS = 1e-6
LOG2E = 1.4426950408889634
ATTN_HEADS = 8
ATTN_KV_HEADS = 2
HEAD_DIM = 64
ATTN_W = ATTN_HEADS * HEAD_DIM
KV_W = ATTN_KV_HEADS * HEAD_DIM
ROPE_THETA = 10000.0
CONV_CH = 256
CONV_WIDTH = 31
CONV_PAD = CONV_WIDTH // 2
MLSTM_HEADS = 4
MLSTM_HEAD_DIM = 64
MLSTM_W = MLSTM_HEADS * MLSTM_HEAD_DIM
MLSTM_CHUNK = 128
N_GATES = 16
XATTN_HEADS = 4
OFF_K = ATTN_W
OFF_V = OFF_K + KV_W
OFF_CONV = OFF_V + KV_W
OFF_MQ = OFF_CONV + 2 * CONV_CH
OFF_MK = OFF_MQ + MLSTM_W
OFF_MV = OFF_MK + MLSTM_W
OFF_MO = OFF_MV + MLSTM_W
OFF_MG = OFF_MO + MLSTM_W

LANES = 128
HALO_ROWS = 16
KEY_CHUNK = 512
BF16_ROWS = 16
VT_ROWS = HEAD_DIM + BF16_ROWS
VMEM_LIMIT = 56 * 1024 * 1024


def _cparams(sem):
    return pltpu.CompilerParams(dimension_semantics=sem, vmem_limit_bytes=VMEM_LIMIT)


def _const_spec(shape):
    nd = len(shape)
    return pl.BlockSpec(shape, lambda *_: (0,) * nd)


def _layer_spec(block2d, layer, **kw):
    return pl.BlockSpec((None,) + tuple(block2d), lambda *_: (layer, 0, 0), **kw)


def _rms(x, g):
    return x * lax.rsqrt(jnp.mean(x * x, axis=-1, keepdims=True) + EPS) * g


def _group_mean(sq, bd):
    hi = sq.astype(BF16)
    lo = (sq - hi.astype(F32)).astype(BF16)
    return (jnp.dot(hi, bd, preferred_element_type=F32)
            + jnp.dot(lo, bd, preferred_element_type=F32))


def _swap_pairs(x):
    nxt = pltpu.roll(x, LANES - 1, axis=1)
    prv = pltpu.roll(x, 1, axis=1)
    lane = lax.broadcasted_iota(jnp.int32, x.shape, 1)
    return jnp.where((lane & 1) == 0, nxt, prv)


def _sigmoid(x):
    return 1.0 / (1.0 + jnp.exp(-x))


def _ffn_body(x_ref, g_ref, w13_ref, w2_ref, o_ref, *, d_ff, n_chunks):
    x = x_ref[...]
    xb = _rms(x, g_ref[...]).astype(BF16)
    fc = d_ff // n_chunks
    acc = None
    for c in range(n_chunks):
        a = jnp.dot(xb, w13_ref[:, c * fc:(c + 1) * fc], preferred_element_type=F32)
        b = jnp.dot(xb, w13_ref[:, d_ff + c * fc:d_ff + (c + 1) * fc], preferred_element_type=F32)
        act = (a * _sigmoid(a) * b).astype(BF16)
        part = jnp.dot(act, w2_ref[c * fc:(c + 1) * fc, :], preferred_element_type=F32)
        acc = part if acc is None else acc + part
    o_ref[...] = x + 0.5 * acc


def _ffn(x2d, g, w13, w2, layer, *, tm, n_chunks):
    n, d = x2d.shape
    d_ff = w2.shape[1]
    return pl.pallas_call(
        functools.partial(_ffn_body, d_ff=d_ff, n_chunks=n_chunks),
        out_shape=jax.ShapeDtypeStruct((n, d), F32),
        grid=(n // tm,),
        in_specs=[
            pl.BlockSpec((tm, d), lambda i: (i, 0)),
            _const_spec((1, d)),
            _layer_spec((d, 2 * d_ff), layer, pipeline_mode=pl.Buffered(1)),
            _layer_spec((d_ff, d), layer, pipeline_mode=pl.Buffered(1)),
        ],
        out_specs=pl.BlockSpec((tm, d), lambda i: (i, 0)),
        compiler_params=_cparams(("parallel",)),
        name="swiglu_half_step",
    )(x2d, g, w13, w2)


def _inproj_body(x_ref, g_ref, w_ref, wg_ref, wgt_ref, gb_ref, gbt_ref, qg_ref, kg_ref,
                 cos_ref, sin_ref, bd_ref,
                 q_ref, k_ref, vt_ref, u_ref, mq_ref, mk_ref, mv_ref, og_ref, gate_ref, gatet_ref):
    xb = _rms(x_ref[0], g_ref[...]).astype(BF16)
    h = jnp.dot(xb, w_ref[...], preferred_element_type=F32)
    bd = bd_ref[...]
    cos = cos_ref[...]
    sin = sin_ref[...]

    hq = h[:, 0:ATTN_W]
    ms = jnp.concatenate([_group_mean(hq[:, 0:256] * hq[:, 0:256], bd),
                          _group_mean(hq[:, 256:512] * hq[:, 256:512], bd)], axis=1)
    qn = hq * lax.rsqrt(ms + EPS) * qg_ref[...]
    in_first = lax.broadcasted_iota(jnp.int32, cos.shape, 1) < HEAD_DIM
    group = ATTN_HEADS // ATTN_KV_HEADS
    for j in range(ATTN_W // LANES):
        z = qn[:, j * LANES:(j + 1) * LANES]
        r = (z * cos + _swap_pairs(z) * sin) * (HEAD_DIM ** -0.5 * LOG2E)
        rolled = pltpu.roll(r, HEAD_DIM, axis=1)
        for half in range(2):
            hd = 2 * j + half
            g = hd // group
            z = r if half == g else rolled
            keep = jnp.where(in_first, z, 0.0) if g == 0 else jnp.where(in_first, 0.0, z)
            q_ref[0, hd] = keep.astype(BF16)

    hk = h[:, OFF_K:OFF_V]
    msk = _group_mean(hk * hk, bd_ref[0:LANES, 0:LANES])
    kn = hk * lax.rsqrt(msk + EPS) * kg_ref[...]
    k_ref[0] = (kn * cos + _swap_pairs(kn) * sin).astype(BF16)
    hv = h[:, OFF_V:OFF_CONV]
    for r in range(hv.shape[0] // KEY_CHUNK):
        vt = hv[r * KEY_CHUNK:(r + 1) * KEY_CHUNK, :].T.astype(BF16)
        for g in range(ATTN_KV_HEADS):
            vt_ref[0, r, g, 0:HEAD_DIM] = vt[g * HEAD_DIM:(g + 1) * HEAD_DIM, :]
            vt_ref[0, r, g, HEAD_DIM:VT_ROWS] = jnp.ones((VT_ROWS - HEAD_DIM, KEY_CHUNK), BF16)

    u_ref[0] = h[:, OFF_CONV:OFF_CONV + CONV_CH] * _sigmoid(h[:, OFF_CONV + CONV_CH:OFF_MQ])

    mq_ref[0] = h[:, OFF_MQ:OFF_MK].astype(BF16)
    mk_ref[0] = (h[:, OFF_MK:OFF_MV] * (MLSTM_HEAD_DIM ** -0.5)).astype(BF16)
    mv_ref[0] = h[:, OFF_MV:OFF_MO].astype(BF16)
    og_ref[0] = _sigmoid(h[:, OFF_MO:OFF_MG])

    def log_gate(pre, is_forget):
        ls = jnp.minimum(pre, 0.0) - jnp.log1p(jnp.exp(-jnp.abs(pre)))
        return jnp.where(is_forget, ls, pre)

    gp = jnp.dot(xb, wg_ref[...], preferred_element_type=F32) + gb_ref[...]
    col = lax.broadcasted_iota(jnp.int32, gp.shape, 1)
    gate_ref[0] = log_gate(gp, ((col >> 2) & 1) == 1)
    gpt = lax.dot_general(wgt_ref[...], xb, (((1,), (1,)), ((), ())),
                          preferred_element_type=F32) + gbt_ref[...]
    row = lax.broadcasted_iota(jnp.int32, gpt.shape, 0)
    gatet_ref[0] = log_gate(gpt, ((row >> 2) & 1) == 1)


def _inproj(x, g, w_in, w_gate, w_gate_t, layer, gate_b, gate_bt, q_gain, k_gain, cos, sin, bd, *, tm):
    b, s, d = x.shape
    tok = lambda width, dt: jax.ShapeDtypeStruct((b, s, width), dt)
    tok_spec = lambda width: pl.BlockSpec((1, tm, width), lambda bi, i: (bi, i, 0))
    q_spec = pl.BlockSpec((1, ATTN_HEADS, tm, LANES), lambda bi, i: (bi, 0, i, 0))
    vt_shape = (b, s // KEY_CHUNK, ATTN_KV_HEADS, VT_ROWS, KEY_CHUNK)
    vt_spec = pl.BlockSpec((1, tm // KEY_CHUNK) + vt_shape[2:], lambda bi, i: (bi, i, 0, 0, 0))
    return pl.pallas_call(
        _inproj_body,
        out_shape=(
            jax.ShapeDtypeStruct((b, ATTN_HEADS, s, LANES), BF16),
            tok(KV_W, BF16),
            jax.ShapeDtypeStruct(vt_shape, BF16),
            tok(CONV_CH, F32),
            tok(MLSTM_W, BF16), tok(MLSTM_W, BF16), tok(MLSTM_W, BF16),
            tok(MLSTM_W, F32),
            tok(LANES, F32),
            jax.ShapeDtypeStruct((b, N_GATES, s), F32),
        ),
        grid=(b, s // tm),
        in_specs=[
            tok_spec(d),
            _const_spec((1, d)),
            _layer_spec((d, OFF_MG), layer),
            _layer_spec((d, LANES), layer),
            _layer_spec((N_GATES, d), layer),
            _const_spec((1, LANES)),
            _const_spec((N_GATES, 1)),
            _const_spec((1, ATTN_W)),
            _const_spec((1, LANES)),
            pl.BlockSpec((tm, LANES), lambda bi, i: (i, 0)),
            pl.BlockSpec((tm, LANES), lambda bi, i: (i, 0)),
            _const_spec(bd.shape),
        ],
        out_specs=(
            q_spec, tok_spec(KV_W), vt_spec, tok_spec(CONV_CH),
            tok_spec(MLSTM_W), tok_spec(MLSTM_W), tok_spec(MLSTM_W), tok_spec(MLSTM_W),
            tok_spec(LANES),
            pl.BlockSpec((1, N_GATES, tm), lambda bi, i: (bi, 0, i)),
        ),
        compiler_params=_cparams(("parallel", "parallel")),
        name="input_projection",
    )(x, g, w_in, w_gate, w_gate_t, gate_b, gate_bt, q_gain, k_gain, cos, sin, bd)


def _attn_body(q_ref, k_ref, vt_ref, o_ref, s_ref, *, unroll):
    tq = q_ref.shape[2]
    n_chunks = vt_ref.shape[1]
    tk = vt_ref.shape[4]
    nt = (((1,), (1,)), ((), ()))

    def scores(c, slot):
        kc = k_ref[0, pl.ds(pl.multiple_of(c * tk, tk), tk), :]
        for j in range(2):
            s_ref[slot, j] = lax.dot_general(kc, q_ref[0, j], nt, preferred_element_type=F32)

    def consume(c, slot, state):
        vtc = vt_ref[0, c, 0]
        out = []
        for j in range(2):
            m, acc = state[j]
            s = s_ref[slot, j]
            m_new = jnp.maximum(m, jnp.max(s, axis=0, keepdims=True))
            alpha = jnp.exp2(m - m_new)
            p = jnp.exp2(s - m_new).astype(BF16)
            out.append((m_new, alpha * acc + jnp.dot(vtc, p, preferred_element_type=F32)))
        return tuple(out)

    def step(i, state):
        for u in range(unroll):
            c = unroll * i + u
            scores(jnp.minimum(c + 1, n_chunks - 1), (u + 1) % 2)
            state = consume(c, u % 2, state)
        return state

    init = tuple((jnp.full((1, tq), -jnp.inf, F32), jnp.zeros((VT_ROWS, tq), F32)) for _ in range(2))
    scores(0, 0)
    (_, a0), (_, a1) = lax.fori_loop(0, n_chunks // unroll, step, init)
    o = [a[0:HEAD_DIM] / a[HEAD_DIM:HEAD_DIM + 1] for a in (a0, a1)]
    o_ref[0] = jnp.concatenate(o, axis=0).T.astype(BF16)


def _attention(q, k, vt, *, tq):
    b, _, s, _ = q.shape
    n_chunks, _, hd, tk = vt.shape[1:]
    unroll = 4 if n_chunks % 4 == 0 else 2
    assert n_chunks % unroll == 0
    heads_per_kv = ATTN_HEADS // ATTN_KV_HEADS
    return pl.pallas_call(
        functools.partial(_attn_body, unroll=unroll),
        out_shape=jax.ShapeDtypeStruct((b, s, ATTN_W), BF16),
        grid=(b, ATTN_HEADS // 2, s // tq),
        in_specs=[pl.BlockSpec((1, 2, tq, LANES), lambda bi, p, i: (bi, p, i, 0)),
                  pl.BlockSpec((1, s, KV_W), lambda bi, p, i: (bi, 0, 0)),
                  pl.BlockSpec((1, n_chunks, 1, hd, tk), lambda bi, p, i: (bi, 0, (2 * p) // heads_per_kv, 0, 0))],
        out_specs=pl.BlockSpec((1, tq, LANES), lambda bi, p, i: (bi, i, p)),
        scratch_shapes=[pltpu.VMEM((2, 2, tk, tq), F32)],
        compiler_params=_cparams(("parallel", "parallel", "arbitrary")),
        name="gqa_attention",
    )(q, k, vt)


def _conv_body(prev_ref, cur_ref, next_ref, w_ref, b_ref, lg_ref, lb_ref, o_ref):
    i = pl.program_id(1)
    ts = cur_ref.shape[1]
    prev = jnp.where(i > 0, prev_ref[0], 0.0)
    nxt = jnp.where(i < pl.num_programs(1) - 1, next_ref[0], 0.0)
    win = jnp.concatenate([prev, cur_ref[0], nxt], axis=0)
    w = w_ref[...]
    acc = jnp.zeros((ts, CONV_CH), F32) + b_ref[...]
    for t in range(CONV_WIDTH):
        off = HALO_ROWS - CONV_PAD + t
        acc = acc + win[off:off + ts, :] * w[t:t + 1, :]
    mu = jnp.mean(acc, axis=-1, keepdims=True)
    cen = acc - mu
    var = jnp.mean(cen * cen, axis=-1, keepdims=True)
    un = cen * lax.rsqrt(var + EPS) * lg_ref[...] + lb_ref[...]
    o_ref[0] = (un * _sigmoid(un)).astype(BF16)


def _conv(u, dw_w, dw_b, ln_g, ln_b, *, ts):
    b, s, c = u.shape
    r = ts // HALO_ROWS
    n_halo = s // HALO_ROWS
    return pl.pallas_call(
        _conv_body,
        out_shape=jax.ShapeDtypeStruct((b, s, c), BF16),
        grid=(b, s // ts),
        in_specs=[
            pl.BlockSpec((1, HALO_ROWS, c), lambda bi, i: (bi, jnp.maximum(i * r - 1, 0), 0)),
            pl.BlockSpec((1, ts, c), lambda bi, i: (bi, i, 0)),
            pl.BlockSpec((1, HALO_ROWS, c), lambda bi, i: (bi, jnp.minimum((i + 1) * r, n_halo - 1), 0)),
            _const_spec((CONV_WIDTH, c)),
            _const_spec((1, c)), _const_spec((1, c)), _const_spec((1, c)),
        ],
        out_specs=pl.BlockSpec((1, ts, c), lambda bi, i: (bi, i, 0)),
        compiler_params=_cparams(("parallel", "parallel")),
        name="conformer_conv",
    )(u, u, u, dw_w, dw_b, ln_g, ln_b)


def _expand_heads(cols):
    rows = cols[0].shape[0]
    hid = lax.broadcasted_iota(jnp.int32, (rows, MLSTM_W), 1) >> 6
    out = jnp.broadcast_to(cols[3], (rows, MLSTM_W))
    for h in (2, 1, 0):
        out = jnp.where(hid == h, cols[h], out)
    return out


def _split3(x):
    a = x.astype(BF16)
    r = x - a.astype(F32)
    b = r.astype(BF16)
    c = (r - b.astype(F32)).astype(BF16)
    return a, b, c


def _mlstm_direction(d, q, k, v, gate, gate_t, c_ref, n_ref, m_ref, bd1):
    L = q.shape[0]
    r_i = lax.broadcasted_iota(jnp.int32, (L, L), 0)
    c_i = lax.broadcasted_iota(jnp.int32, (L, L), 1)
    keep = (c_i <= r_i) if d == 0 else (c_i >= r_i)
    tri = jnp.where(keep, 1.0, 0.0).astype(BF16)
    tri_t = jnp.where((r_i <= c_i) if d == 0 else (r_i >= c_i), 1.0, 0.0).astype(BF16)

    cum_col = sum(jnp.dot(tri, t, preferred_element_type=F32) for t in _split3(gate))
    cum_row = sum(jnp.dot(t, tri_t, preferred_element_type=F32) for t in _split3(gate_t))
    tot = jnp.sum(gate, axis=0, keepdims=True)

    hid = lax.broadcasted_iota(jnp.int32, (L, MLSTM_W), 1) >> 6
    m_prev_l = m_ref[d]
    qf = q.astype(F32)
    kf = k.astype(F32)

    m_t_c, inter_c, rs_c, wend_c, mnew_c, decay_c = [], [], [], [], [], []
    num_intra = jnp.zeros((L, MLSTM_W), F32)
    for h in range(MLSTM_HEADS):
        ji = d * 8 + h
        jf = d * 8 + MLSTM_HEADS + h
        m_prev = m_prev_l[:, h * MLSTM_HEAD_DIM:h * MLSTM_HEAD_DIM + 1]
        b_col = cum_col[:, jf:jf + 1]
        i_col = gate[:, ji:ji + 1]
        b_row = cum_row[jf:jf + 1, :]
        i_row = gate_t[ji:ji + 1, :]
        b_end = tot[:, jf:jf + 1]

        logw = jnp.where(keep, b_col - b_row + i_row, -jnp.inf)
        m_inter = b_col + m_prev
        m_t = jnp.maximum(jnp.max(logw, axis=-1, keepdims=True), m_inter)
        w = jnp.exp(logw - m_t)
        qm = jnp.where(hid == h, qf, 0.0).astype(BF16)
        s = lax.dot_general(qm, k, (((1,), (1,)), ((), ())), preferred_element_type=F32) * w
        r = jnp.dot(s.astype(BF16), v, preferred_element_type=F32)
        num_intra = jnp.where(hid == h, r, num_intra)
        m_t_c.append(m_t)
        inter_c.append(jnp.exp(m_inter - m_t))
        rs_c.append(jnp.sum(s, axis=-1, keepdims=True))

        logw_end = b_end - b_col + i_col
        m_new = jnp.maximum(b_end + m_prev, jnp.max(logw_end, axis=0, keepdims=True))
        wend_c.append(jnp.exp(logw_end - m_new))
        mnew_c.append(m_new)
        decay_c.append(jnp.exp(b_end + m_prev - m_new))

    m_t_l = _expand_heads(m_t_c)
    inter_l = _expand_heads(inter_c)
    c_mat = c_ref[d]
    n_vec = n_ref[d]
    c_hi = c_mat.astype(BF16)
    c_lo = (c_mat - c_hi.astype(F32)).astype(BF16)
    qc = (jnp.dot(q, c_hi, preferred_element_type=F32)
          + jnp.dot(q, c_lo, preferred_element_type=F32))
    qn_l = _group_mean(qf * n_vec, bd1)
    num = num_intra + inter_l * qc
    den = _expand_heads(rs_c) + inter_l * qn_l
    h_out = num / jnp.maximum(jnp.abs(den), jnp.exp(-m_t_l))

    decay_l = _expand_heads(decay_c)
    kw = kf * _expand_heads(wend_c)
    kv = lax.dot_general(kw.astype(BF16), v, (((0,), (0,)), ((), ())),
                         preferred_element_type=F32)
    rr = lax.broadcasted_iota(jnp.int32, (MLSTM_W, MLSTM_W), 0) >> 6
    cc = lax.broadcasted_iota(jnp.int32, (MLSTM_W, MLSTM_W), 1) >> 6
    c_ref[d] = decay_l * c_mat + jnp.where(rr == cc, kv, 0.0)
    n_ref[d] = decay_l * n_vec + jnp.sum(kw, axis=0, keepdims=True)
    m_ref[d] = _expand_heads(mnew_c)
    return h_out


def _mlstm_body(qf_ref, kf_ref, vf_ref, gf_ref, gtf_ref, qb_ref, kb_ref, vb_ref, gb_ref, gtb_ref,
                bd1_ref, hf_ref, hb_ref, c_ref, n_ref, m_ref):
    @pl.when(pl.program_id(1) == 0)
    def _():
        c_ref[...] = jnp.zeros_like(c_ref)
        n_ref[...] = jnp.zeros_like(n_ref)
        m_ref[...] = jnp.zeros_like(m_ref)

    bd1 = bd1_ref[...]
    hf_ref[0] = _mlstm_direction(0, qf_ref[0], kf_ref[0], vf_ref[0], gf_ref[0], gtf_ref[0],
                                 c_ref, n_ref, m_ref, bd1)
    hb_ref[0] = _mlstm_direction(1, qb_ref[0], kb_ref[0], vb_ref[0], gb_ref[0], gtb_ref[0],
                                 c_ref, n_ref, m_ref, bd1)


def _mlstm(mq, mk, mv, gate, gate_t, bd1):
    b, s, w = mq.shape
    L = MLSTM_CHUNK
    nc = s // L
    fwd = pl.BlockSpec((1, L, w), lambda bi, c: (bi, c, 0))
    bwd = pl.BlockSpec((1, L, w), lambda bi, c: (bi, nc - 1 - c, 0))
    gfwd = pl.BlockSpec((1, L, LANES), lambda bi, c: (bi, c, 0))
    gbwd = pl.BlockSpec((1, L, LANES), lambda bi, c: (bi, nc - 1 - c, 0))
    gtfwd = pl.BlockSpec((1, N_GATES, L), lambda bi, c: (bi, 0, c))
    gtbwd = pl.BlockSpec((1, N_GATES, L), lambda bi, c: (bi, 0, nc - 1 - c))
    return pl.pallas_call(
        _mlstm_body,
        out_shape=(jax.ShapeDtypeStruct((b, s, w), F32), jax.ShapeDtypeStruct((b, s, w), F32)),
        grid=(b, nc),
        in_specs=[fwd, fwd, fwd, gfwd, gtfwd, bwd, bwd, bwd, gbwd, gtbwd, _const_spec(bd1.shape)],
        out_specs=(fwd, bwd),
        scratch_shapes=[pltpu.VMEM((2, w, w), F32), pltpu.VMEM((2, 1, w), F32), pltpu.VMEM((2, 1, w), F32)],
        compiler_params=_cparams(("parallel", "arbitrary")),
        name="mlstm_scan",
    )(mq, mk, mv, gate, gate_t, mq, mk, mv, gate, gate_t, bd1)


def _memkv_body(mem_ref, g_ref, w_ref, kg_ref, k_ref, v_ref):
    mb = _rms(mem_ref[0], g_ref[...]).astype(BF16)
    kv = jnp.dot(mb, w_ref[...], preferred_element_type=F32)
    d = kv.shape[1] // 2
    hd = d // XATTN_HEADS
    for h in range(XATTN_HEADS):
        k_ref[0, :, h * hd:(h + 1) * hd] = _rms(kv[:, h * hd:(h + 1) * hd], kg_ref[...]).astype(BF16)
    v_ref[0] = kv[:, d:].astype(BF16)


def _memkv(mem, g, wkv, layer, k_gain):
    b, m, d = mem.shape
    spec = pl.BlockSpec((1, m, d), lambda bi: (bi, 0, 0))
    return pl.pallas_call(
        _memkv_body,
        out_shape=(jax.ShapeDtypeStruct((b, m, d), BF16), jax.ShapeDtypeStruct((b, m, d), BF16)),
        grid=(b,),
        in_specs=[spec, _const_spec((1, d)), _layer_spec(wkv.shape[1:], layer), _const_spec(k_gain.shape)],
        out_specs=(spec, spec),
        compiler_params=_cparams(("parallel",)),
        name="memory_kv",
    )(mem, g, wkv, k_gain)


def _mix_out_body(x_ref, ya_ref, yc_ref, hf_ref, hb_ref, og_ref, mg_ref, bd_ref, wo_ref,
                  xg_ref, wq_ref, qg_ref, k_ref, v_ref, wxo_ref, o_ref):
    hs = hf_ref[0] + hb_ref[0]
    hn = hs * lax.rsqrt(_group_mean(hs * hs, bd_ref[...]) + EPS) * mg_ref[...]
    ym = (og_ref[0] * hn).astype(BF16)
    x = x_ref[0]
    x = (x + jnp.dot(ya_ref[0], wo_ref[0:ATTN_W, :], preferred_element_type=F32)
         + jnp.dot(yc_ref[0], wo_ref[ATTN_W:ATTN_W + CONV_CH, :], preferred_element_type=F32)
         + jnp.dot(ym, wo_ref[ATTN_W + CONV_CH:, :], preferred_element_type=F32))

    xb = _rms(x, xg_ref[...]).astype(BF16)
    qx = jnp.dot(xb, wq_ref[...], preferred_element_type=F32)
    hd = qx.shape[1] // XATTN_HEADS
    outs = []
    for h in range(XATTN_HEADS):
        sl = slice(h * hd, (h + 1) * hd)
        qh = (_rms(qx[:, sl], qg_ref[...]) * (hd ** -0.5)).astype(BF16)
        s = lax.dot_general(qh, k_ref[0, :, sl], (((1,), (1,)), ((), ())), preferred_element_type=F32)
        p = jnp.exp(s - jnp.max(s, axis=-1, keepdims=True))
        l = jnp.sum(p, axis=-1, keepdims=True)
        outs.append((jnp.dot(p.astype(BF16), v_ref[0, :, sl], preferred_element_type=F32) / l).astype(BF16))
    o = jnp.concatenate(outs, axis=1)
    o_ref[0] = x + jnp.dot(o, wxo_ref[...], preferred_element_type=F32)


def _mix_out(x, ya, yc, hf, hb, og, m_gain, bd, w_out, x_gain, wq, q_gain, mk, mv, wo, layer, *, tm):
    b, s, d = x.shape
    tok = lambda width: pl.BlockSpec((1, tm, width), lambda bi, i: (bi, i, 0))
    mem_spec = pl.BlockSpec((1,) + mk.shape[1:], lambda bi, i: (bi, 0, 0))
    return pl.pallas_call(
        _mix_out_body,
        out_shape=jax.ShapeDtypeStruct((b, s, d), F32),
        grid=(b, s // tm),
        in_specs=[tok(d), tok(ATTN_W), tok(CONV_CH), tok(MLSTM_W), tok(MLSTM_W), tok(MLSTM_W),
                  _const_spec(m_gain.shape), _const_spec(bd.shape), _layer_spec(w_out.shape[1:], layer),
                  _const_spec(x_gain.shape), _layer_spec(wq.shape[1:], layer), _const_spec(q_gain.shape),
                  mem_spec, mem_spec, _layer_spec(wo.shape[1:], layer)],
        out_specs=tok(d),
        compiler_params=_cparams(("parallel", "parallel")),
        name="mix_out_cross_attention",
    )(x, ya, yc, hf, hb, og, m_gain, bd, w_out, x_gain, wq, q_gain, mk, mv, wo)


def _rope_tables(s_len):
    rows = s_len // GRID_W
    row_idx = jnp.repeat(jnp.arange(rows, dtype=jnp.int32), GRID_W).astype(F32)
    col_idx = jnp.tile(jnp.arange(GRID_W, dtype=jnp.int32), rows).astype(F32)
    n_freq = HEAD_DIM // 4
    inv_freq = jnp.float32(ROPE_THETA) ** (-jnp.arange(n_freq, dtype=F32) / n_freq)
    ang = jnp.concatenate([row_idx[:, None] * inv_freq, col_idx[:, None] * inv_freq], axis=-1)
    cos = jnp.repeat(jnp.cos(ang), 2, axis=-1)
    sin = jnp.repeat(jnp.sin(ang), 2, axis=-1) * jnp.tile(jnp.array([-1.0, 1.0], F32), HEAD_DIM // 2)
    return jnp.tile(cos, (1, 2)), jnp.tile(sin, (1, 2))


def _block_diag(scale):
    idx = jnp.arange(MLSTM_W) // MLSTM_HEAD_DIM
    return jnp.where(idx[:, None] == idx[None, :], scale, 0.0).astype(BF16)


def _pick_tile(n, pref):
    t = min(n, pref)
    while n % t:
        t //= 2
    return t


def kernel(x, mem, ffn1_norm, ffn1_w13, ffn1_w2, mix_norm, w_in, attn_q_norm, attn_k_norm, conv_dw_w,
           conv_dw_b, conv_ln_g, conv_ln_b, mlstm_gate_b, mlstm_out_norm, w_out, xattn_norm, mem_norm,
           xattn_wq, xattn_wkv, xattn_q_norm, xattn_k_norm, xattn_wo, ffn2_norm, ffn2_w13, ffn2_w2):
    b, s, d = x.shape
    depth = w_in.shape[0]
    cos, sin = _rope_tables(s)
    bd_mean = _block_diag(1.0 / MLSTM_HEAD_DIM)
    bd_sum = _block_diag(1.0)

    tm_ffn = _pick_tile(b * s, 512)
    tm_tok = _pick_tile(s, 512)
    tq = _pick_tile(s, 512)
    ts = _pick_tile(s, 256)
    row = lambda v: v.reshape(1, -1)

    bf = lambda w: w.astype(BF16)
    ffn1_w13, ffn1_w2, ffn2_w13, ffn2_w2 = bf(ffn1_w13), bf(ffn1_w2), bf(ffn2_w13), bf(ffn2_w2)
    w_out, xattn_wq, xattn_wkv, xattn_wo = bf(w_out), bf(xattn_wq), bf(xattn_wkv), bf(xattn_wo)
    w_gate = w_in[:, :, OFF_MG:]
    w_gate_pad = bf(jnp.pad(w_gate, ((0, 0), (0, 0), (0, LANES - N_GATES))))
    w_gate_t = bf(jnp.swapaxes(w_gate, 1, 2))
    w_in = bf(w_in)

    for l in range(depth):
        x = _ffn(x.reshape(b * s, d), row(ffn1_norm[l]), ffn1_w13, ffn1_w2, l,
                 tm=tm_ffn, n_chunks=2).reshape(b, s, d)

        gate_b = mlstm_gate_b[l].reshape(1, N_GATES)
        q, kk, vt, u, mq, mk, mv, og, gate, gate_t = _inproj(
            x, row(mix_norm[l]), w_in, w_gate_pad, w_gate_t, l,
            jnp.pad(gate_b, ((0, 0), (0, LANES - N_GATES))), gate_b.reshape(N_GATES, 1),
            row(jnp.tile(attn_q_norm[l], ATTN_HEADS)), row(jnp.tile(attn_k_norm[l], ATTN_KV_HEADS)),
            cos, sin, bd_mean, tm=tm_tok)

        ya = _attention(q, kk, vt, tq=tq)
        yc = _conv(u, conv_dw_w[l], row(conv_dw_b[l]), row(conv_ln_g[l]), row(conv_ln_b[l]), ts=ts)
        hf, hb = _mlstm(mq, mk, mv, gate, gate_t, bd_sum)
        mem_k, mem_v = _memkv(mem, row(mem_norm[l]), xattn_wkv, l, row(xattn_k_norm[l]))
        x = _mix_out(x, ya, yc, hf, hb, og, row(mlstm_out_norm[l]), bd_mean, w_out,
                     row(xattn_norm[l]), xattn_wq, row(xattn_q_norm[l]), mem_k, mem_v, xattn_wo, l, tm=tm_tok)

        x = _ffn(x.reshape(b * s, d), row(ffn2_norm[l]), ffn2_w13, ffn2_w2, l,
                 tm=tm_ffn, n_chunks=2).reshape(b, s, d)
    return x
```

```python
import functools

import jax
import jax.numpy as jnp
from jax import lax
from jax.experimental import pallas as pl
from jax.experimental.pallas import tpu as pltpu

F32 = jnp.float32
BF16 = jnp.bfloat16

GRID_W = 64
EPS = 1e-6
LOG2E = 1.4426950408889634
ATTN_HEADS = 8
ATTN_KV_HEADS = 2
HEAD_DIM = 64
ATTN_W = ATTN_HEADS * HEAD_DIM
KV_W = ATTN_KV_HEADS * HEAD_DIM
ROPE_THETA = 10000.0
CONV_CH = 256
CONV_WIDTH = 31
CONV_PAD = CONV_WIDTH // 2
MLSTM_HEADS = 4
MLSTM_HEAD_DIM = 64
MLSTM_W = MLSTM_HEADS * MLSTM_HEAD_DIM
MLSTM_CHUNK = 128
N_GATES = 16
XATTN_HEADS = 4
OFF_K = ATTN_W
OFF_V = OFF_K + KV_W
OFF_CONV = OFF_V + KV_W
OFF_MQ = OFF_CONV + 2 * CONV_CH
OFF_MK = OFF_MQ + MLSTM_W
OFF_MV = OFF_MK + MLSTM_W
OFF_MO = OFF_MV + MLSTM_W
OFF_MG = OFF_MO + MLSTM_W

LANES = 128
SUBLANES = 8
HALO_ROWS = 16
KEY_CHUNK = 512
BF16_ROWS = 16
VT_ROWS = HEAD_DIM + BF16_ROWS
VMEM_LIMIT = 56 * 1024 * 1024


def _cparams(sem):
    return pltpu.CompilerParams(dimension_semantics=sem, vmem_limit_bytes=VMEM_LIMIT)


def _const_spec(shape):
    nd = len(shape)
    return pl.BlockSpec(shape, lambda *_: (0,) * nd)


def _layer_spec(block2d, layer, **kw):
    return pl.BlockSpec((None,) + tuple(block2d), lambda *_: (layer, 0, 0), **kw)


def _rms(x, g):
    return x * lax.rsqrt(jnp.mean(x * x, axis=-1, keepdims=True) + EPS) * g


def _group_mean(sq, bd):
    hi = sq.astype(BF16)
    lo = (sq - hi.astype(F32)).astype(BF16)
    return (jnp.dot(hi, bd, preferred_element_type=F32)
            + jnp.dot(lo, bd, preferred_element_type=F32))


def _swap_pairs(x):
    nxt = pltpu.roll(x, LANES - 1, axis=1)
    prv = pltpu.roll(x, 1, axis=1)
    lane = lax.broadcasted_iota(jnp.int32, x.shape, 1)
    return jnp.where((lane & 1) == 0, nxt, prv)


def _sigmoid(x):
    return 1.0 / (1.0 + jnp.exp(-x))


def _ffn_body(x_ref, g_ref, w13_ref, w2_ref, o_ref, *, d_ff, n_chunks):
    x = x_ref[...]
    xb = _rms(x, g_ref[...]).astype(BF16)
    fc = d_ff // n_chunks
    acc = None
    for c in range(n_chunks):
        a = jnp.dot(xb, w13_ref[:, c * fc:(c + 1) * fc], preferred_element_type=F32)
        b = jnp.dot(xb, w13_ref[:, d_ff + c * fc:d_ff + (c + 1) * fc], preferred_element_type=F32)
        act = (a * _sigmoid(a) * b).astype(BF16)
        part = jnp.dot(act, w2_ref[c * fc:(c + 1) * fc, :], preferred_element_type=F32)
        acc = part if acc is None else acc + part
    o_ref[...] = x + 0.5 * acc


def _ffn(x2d, g, w13, w2, layer, *, tm, n_chunks):
    n, d = x2d.shape
    d_ff = w2.shape[1]
    return pl.pallas_call(
        functools.partial(_ffn_body, d_ff=d_ff, n_chunks=n_chunks),
        out_shape=jax.ShapeDtypeStruct((n, d), F32),
        grid=(n // tm,),
        in_specs=[
            pl.BlockSpec((tm, d), lambda i: (i, 0)),
            _const_spec((1, d)),
            _layer_spec((d, 2 * d_ff), layer, pipeline_mode=pl.Buffered(1)),
            _layer_spec((d_ff, d), layer, pipeline_mode=pl.Buffered(1)),
        ],
        out_specs=pl.BlockSpec((tm, d), lambda i: (i, 0)),
        compiler_params=_cparams(("parallel",)),
        name="swiglu_half_step",
    )(x2d, g, w13, w2)


def _inproj_body(x_ref, g_ref, w_ref, wg_ref, wgt_ref, gb_ref, gbt_ref, qg_ref, kg_ref,
                 cos_ref, sin_ref, bd_ref,
                 q_ref, k_ref, vt_ref, u_ref, mq_ref, mk_ref, mv_ref, og_ref, gate_ref, gatet_ref):
    xb = _rms(x_ref[0], g_ref[...]).astype(BF16)
    h = jnp.dot(xb, w_ref[...], preferred_element_type=F32)
    bd = bd_ref[...]
    cos = cos_ref[...]
    sin = sin_ref[...]

    hq = h[:, 0:ATTN_W]
    ms = jnp.concatenate([_group_mean(hq[:, 0:256] * hq[:, 0:256], bd),
                          _group_mean(hq[:, 256:512] * hq[:, 256:512], bd)], axis=1)
    qn = hq * lax.rsqrt(ms + EPS) * qg_ref[...]
    in_first = lax.broadcasted_iota(jnp.int32, cos.shape, 1) < HEAD_DIM
    group = ATTN_HEADS // ATTN_KV_HEADS
    for j in range(ATTN_W // LANES):
        z = qn[:, j * LANES:(j + 1) * LANES]
        r = (z * cos + _swap_pairs(z) * sin) * (HEAD_DIM ** -0.5 * LOG2E)
        rolled = pltpu.roll(r, HEAD_DIM, axis=1)
        for half in range(2):
            hd = 2 * j + half
            g = hd // group
            z = r if half == g else rolled
            keep = jnp.where(in_first, z, 0.0) if g == 0 else jnp.where(in_first, 0.0, z)
            q_ref[0, hd] = keep.astype(BF16)

    hk = h[:, OFF_K:OFF_V]
    msk = _group_mean(hk * hk, bd_ref[0:LANES, 0:LANES])
    kn = hk * lax.rsqrt(msk + EPS) * kg_ref[...]
    k_ref[0] = (kn * cos + _swap_pairs(kn) * sin).astype(BF16)
    hv = h[:, OFF_V:OFF_CONV]
    for r in range(hv.shape[0] // KEY_CHUNK):
        vt = hv[r * KEY_CHUNK:(r + 1) * KEY_CHUNK, :].T.astype(BF16)
        for g in range(ATTN_KV_HEADS):
            vt_ref[0, r, g, 0:HEAD_DIM] = vt[g * HEAD_DIM:(g + 1) * HEAD_DIM, :]
            vt_ref[0, r, g, HEAD_DIM:VT_ROWS] = jnp.ones((VT_ROWS - HEAD_DIM, KEY_CHUNK), BF16)

    u_ref[0] = h[:, OFF_CONV:OFF_CONV + CONV_CH] * _sigmoid(h[:, OFF_CONV + CONV_CH:OFF_MQ])

    mq_ref[0] = h[:, OFF_MQ:OFF_MK].astype(BF16)
    mk_ref[0] = (h[:, OFF_MK:OFF_MV] * (MLSTM_HEAD_DIM ** -0.5)).astype(BF16)
    mv_ref[0] = h[:, OFF_MV:OFF_MO].astype(BF16)
    og_ref[0] = _sigmoid(h[:, OFF_MO:OFF_MG])

    def log_gate(pre, is_forget):
        ls = jnp.minimum(pre, 0.0) - jnp.log1p(jnp.exp(-jnp.abs(pre)))
        return jnp.where(is_forget, ls, pre)

    gp = jnp.dot(xb, wg_ref[...], preferred_element_type=F32) + gb_ref[...]
    col = lax.broadcasted_iota(jnp.int32, gp.shape, 1)
    gate_ref[0] = log_gate(gp, ((col >> 2) & 1) == 1)
    gpt = lax.dot_general(wgt_ref[...], xb, (((1,), (1,)), ((), ())),
                          preferred_element_type=F32) + gbt_ref[...]
    row = lax.broadcasted_iota(jnp.int32, gpt.shape, 0)
    gatet_ref[0] = log_gate(gpt, ((row >> 2) & 1) == 1)


def _inproj(x, g, w_in, w_gate, w_gate_t, layer, gate_b, gate_bt, q_gain, k_gain, cos, sin, bd, *, tm):
    b, s, d = x.shape
    tok = lambda width, dt: jax.ShapeDtypeStruct((b, s, width), dt)
    tok_spec = lambda width: pl.BlockSpec((1, tm, width), lambda bi, i: (bi, i, 0))
    q_spec = pl.BlockSpec((1, ATTN_HEADS, tm, LANES), lambda bi, i: (bi, 0, i, 0))
    vt_shape = (b, s // KEY_CHUNK, ATTN_KV_HEADS, VT_ROWS, KEY_CHUNK)
    vt_spec = pl.BlockSpec((1, tm // KEY_CHUNK) + vt_shape[2:], lambda bi, i: (bi, i, 0, 0, 0))
    return pl.pallas_call(
        _inproj_body,
        out_shape=(
            jax.ShapeDtypeStruct((b, ATTN_HEADS, s, LANES), BF16),
            tok(KV_W, BF16),
            jax.ShapeDtypeStruct(vt_shape, BF16),
            tok(CONV_CH, F32),
            tok(MLSTM_W, BF16), tok(MLSTM_W, BF16), tok(MLSTM_W, BF16),
            tok(MLSTM_W, F32),
            tok(LANES, F32),
            jax.ShapeDtypeStruct((b, N_GATES, s), F32),
        ),
        grid=(b, s // tm),
        in_specs=[
            tok_spec(d),
            _const_spec((1, d)),
            _layer_spec((d, OFF_MG), layer),
            _layer_spec((d, LANES), layer),
            _layer_spec((N_GATES, d), layer),
            _const_spec((1, LANES)),
            _const_spec((N_GATES, 1)),
            _const_spec((1, ATTN_W)),
            _const_spec((1, LANES)),
            pl.BlockSpec((tm, LANES), lambda bi, i: (i, 0)),
            pl.BlockSpec((tm, LANES), lambda bi, i: (i, 0)),
            _const_spec(bd.shape),
        ],
        out_specs=(
            q_spec, tok_spec(KV_W), vt_spec, tok_spec(CONV_CH),
            tok_spec(MLSTM_W), tok_spec(MLSTM_W), tok_spec(MLSTM_W), tok_spec(MLSTM_W),
            tok_spec(LANES),
            pl.BlockSpec((1, N_GATES, tm), lambda bi, i: (bi, 0, i)),
        ),
        compiler_params=_cparams(("parallel", "parallel")),
        name="input_projection",
    )(x, g, w_in, w_gate, w_gate_t, gate_b, gate_bt, q_gain, k_gain, cos, sin, bd)


def _attn_body(q_ref, k_ref, vt_ref, o_ref, s_ref, mx_ref, *, unroll):
    tq = q_ref.shape[2]
    n_chunks = vt_ref.shape[1]
    tk = vt_ref.shape[4]
    nt = (((1,), (1,)), ((), ()))

    def scores(c, slot):
        kc = k_ref[0, pl.ds(pl.multiple_of(c * tk, tk), tk), :]
        for j in range(2):
            s = lax.dot_general(kc, q_ref[0, j], nt, preferred_element_type=F32)
            s_ref[slot, j] = s
            mx_ref[slot, j] = jnp.max(s, axis=0, keepdims=True)

    def consume(c, slot, state):
        vtc = vt_ref[0, c, 0]
        out = []
        for j in range(2):
            m, acc = state[j]
            s = s_ref[slot, j]
            m_new = jnp.maximum(m, mx_ref[slot, j])
            alpha = jnp.exp2(m - m_new)
            p = jnp.exp2(s - m_new).astype(BF16)
            out.append((m_new, alpha * acc + jnp.dot(vtc, p, preferred_element_type=F32)))
        return tuple(out)

    def step(i, state):
        for u in range(unroll):
            c = unroll * i + u
            scores(jnp.minimum(c + 1, n_chunks - 1), (u + 1) % 2)
            state = consume(c, u % 2, state)
        return state

    init = tuple((jnp.full((1, tq), -jnp.inf, F32), jnp.zeros((VT_ROWS, tq), F32)) for _ in range(2))
    scores(0, 0)
    (_, a0), (_, a1) = lax.fori_loop(0, n_chunks // unroll, step, init)
    o = [a[0:HEAD_DIM] / a[HEAD_DIM:HEAD_DIM + 1] for a in (a0, a1)]
    o_ref[0] = jnp.concatenate(o, axis=0).T.astype(BF16)


def _attention(q, k, vt, *, tq):
    b, _, s, _ = q.shape
    n_chunks, _, hd, tk = vt.shape[1:]
    unroll = 4 if n_chunks % 4 == 0 else 2
    assert n_chunks % unroll == 0
    heads_per_kv = ATTN_HEADS // ATTN_KV_HEADS
    return pl.pallas_call(
        functools.partial(_attn_body, unroll=unroll),
        out_shape=jax.ShapeDtypeStruct((b, s, ATTN_W), BF16),
        grid=(b, ATTN_HEADS // 2, s // tq),
        in_specs=[pl.BlockSpec((1, 2, tq, LANES), lambda bi, p, i: (bi, p, i, 0)),
                  pl.BlockSpec((1, s, KV_W), lambda bi, p, i: (bi, 0, 0)),
                  pl.BlockSpec((1, n_chunks, 1, hd, tk), lambda bi, p, i: (bi, 0, (2 * p) // heads_per_kv, 0, 0))],
        out_specs=pl.BlockSpec((1, tq, LANES), lambda bi, p, i: (bi, i, p)),
        scratch_shapes=[pltpu.VMEM((2, 2, tk, tq), F32), pltpu.VMEM((2, 2, 1, tq), F32)],
        compiler_params=_cparams(("parallel", "parallel", "arbitrary")),
        name="gqa_attention",
    )(q, k, vt)


def _conv_body(prev_ref, cur_ref, next_ref, w_ref, b_ref, lg_ref, lb_ref, o_ref, shift_ref):
    i = pl.program_id(1)
    ts = cur_ref.shape[1]
    prev = jnp.where(i > 0, prev_ref[0], 0.0)
    nxt = jnp.where(i < pl.num_programs(1) - 1, next_ref[0], 0.0)
    win = jnp.concatenate([prev, cur_ref[0], nxt], axis=0)
    w = w_ref[...]
    acc = jnp.zeros((ts, CONV_CH), F32) + b_ref[...]
    first_off = HALO_ROWS - CONV_PAD
    span = shift_ref.shape[0] - ts
    assert span >= ((first_off + CONV_WIDTH - 1) // SUBLANES) * SUBLANES
    assert SUBLANES - 1 + ts + span <= win.shape[0]
    for r in range(SUBLANES):
        shift_ref[...] = win[r:r + ts + span, :]
        for t in range(CONV_WIDTH):
            off = first_off + t
            if off % SUBLANES == r:
                a = off - r
                acc = acc + shift_ref[a:a + ts, :] * w[t:t + 1, :]
    mu = jnp.mean(acc, axis=-1, keepdims=True)
    cen = acc - mu
    var = jnp.mean(cen * cen, axis=-1, keepdims=True)
    un = cen * lax.rsqrt(var + EPS) * lg_ref[...] + lb_ref[...]
    o_ref[0] = (un * _sigmoid(un)).astype(BF16)


def _conv(u, dw_w, dw_b, ln_g, ln_b, *, ts):
    b, s, c = u.shape
    r = ts // HALO_ROWS
    n_halo = s // HALO_ROWS
    return pl.pallas_call(
        _conv_body,
        out_shape=jax.ShapeDtypeStruct((b, s, c), BF16),
        grid=(b, s // ts),
        in_specs=[
            pl.BlockSpec((1, HALO_ROWS, c), lambda bi, i: (bi, jnp.maximum(i * r - 1, 0), 0)),
            pl.BlockSpec((1, ts, c), lambda bi, i: (bi, i, 0)),
            pl.BlockSpec((1, HALO_ROWS, c), lambda bi, i: (bi, jnp.minimum((i + 1) * r, n_halo - 1), 0)),
            _const_spec((CONV_WIDTH, c)),
            _const_spec((1, c)), _const_spec((1, c)), _const_spec((1, c)),
        ],
        out_specs=pl.BlockSpec((1, ts, c), lambda bi, i: (bi, i, 0)),
        scratch_shapes=[pltpu.VMEM((ts + 2 * HALO_ROWS - SUBLANES, c), F32)],
        compiler_params=_cparams(("parallel", "parallel")),
        name="conformer_conv",
    )(u, u, u, dw_w, dw_b, ln_g, ln_b)


def _expand_heads(cols):
    rows = cols[0].shape[0]
    hid = lax.broadcasted_iota(jnp.int32, (rows, MLSTM_W), 1) >> 6
    out = jnp.broadcast_to(cols[3], (rows, MLSTM_W))
    for h in (2, 1, 0):
        out = jnp.where(hid == h, cols[h], out)
    return out


def _split3(x):
    a = x.astype(BF16)
    r = x - a.astype(F32)
    b = r.astype(BF16)
    c = (r - b.astype(F32)).astype(BF16)
    return a, b, c


def _mlstm_direction(d, q, k, v, gate, gate_t, c_ref, n_ref, m_ref, bd1):
    si = d
    L = q.shape[0]
    r_i = lax.broadcasted_iota(jnp.int32, (L, L), 0)
    c_i = lax.broadcasted_iota(jnp.int32, (L, L), 1)
    keep = (c_i <= r_i) if d == 0 else (c_i >= r_i)
    tri = jnp.where(keep, 1.0, 0.0).astype(BF16)
    tri_t = jnp.where((r_i <= c_i) if d == 0 else (r_i >= c_i), 1.0, 0.0).astype(BF16)

    cum_col = sum(jnp.dot(tri, t, preferred_element_type=F32) for t in _split3(gate))
    cum_row = sum(jnp.dot(t, tri_t, preferred_element_type=F32) for t in _split3(gate_t))
    tot = jnp.sum(gate, axis=0, keepdims=True)

    hid = lax.broadcasted_iota(jnp.int32, (L, MLSTM_W), 1) >> 6
    m_prev_l = m_ref[si]
    qf = q.astype(F32)
    kf = k.astype(F32)

    m_t_c, inter_c, rs_c, wend_c, mnew_c, decay_c = [], [], [], [], [], []
    num_intra = jnp.zeros((L, MLSTM_W), F32)
    for h in range(MLSTM_HEADS):
        ji = d * 8 + h
        jf = d * 8 + MLSTM_HEADS + h
        m_prev = m_prev_l[:, h * MLSTM_HEAD_DIM:h * MLSTM_HEAD_DIM + 1]
        b_col = cum_col[:, jf:jf + 1]
        i_col = gate[:, ji:ji + 1]
        b_row = cum_row[jf:jf + 1, :]
        i_row = gate_t[ji:ji + 1, :]
        b_end = tot[:, jf:jf + 1]

        logw = jnp.where(keep, b_col - b_row + i_row, -jnp.inf)
        m_inter = b_col + m_prev
        m_t = jnp.maximum(jnp.max(logw, axis=-1, keepdims=True), m_inter)
        w = jnp.exp(logw - m_t)
        qm = jnp.where(hid == h, qf, 0.0).astype(BF16)
        s = lax.dot_general(qm, k, (((1,), (1,)), ((), ())), preferred_element_type=F32) * w
        r = jnp.dot(s.astype(BF16), v, preferred_element_type=F32)
        num_intra = jnp.where(hid == h, r, num_intra)
        m_t_c.append(m_t)
        inter_c.append(jnp.exp(m_inter - m_t))
        rs_c.append(jnp.sum(s, axis=-1, keepdims=True))

        logw_end = b_end - b_col + i_col
        m_new = jnp.maximum(b_end + m_prev, jnp.max(logw_end, axis=0, keepdims=True))
        wend_c.append(jnp.exp(logw_end - m_new))
        mnew_c.append(m_new)
        decay_c.append(jnp.exp(b_end + m_prev - m_new))

    m_t_l = _expand_heads(m_t_c)
    inter_l = _expand_heads(inter_c)
    c_mat = c_ref[si]
    n_vec = n_ref[si]
    c_hi = c_mat.astype(BF16)
    c_lo = (c_mat - c_hi.astype(F32)).astype(BF16)
    qc = (jnp.dot(q, c_hi, preferred_element_type=F32)
          + jnp.dot(q, c_lo, preferred_element_type=F32))
    qn_l = _group_mean(qf * n_vec, bd1)
    num = num_intra + inter_l * qc
    den = _expand_heads(rs_c) + inter_l * qn_l
    h_out = num / jnp.maximum(jnp.abs(den), jnp.exp(-m_t_l))

    decay_l = _expand_heads(decay_c)
    kw = kf * _expand_heads(wend_c)
    kv = lax.dot_general(kw.astype(BF16), v, (((0,), (0,)), ((), ())),
                         preferred_element_type=F32)
    rr = lax.broadcasted_iota(jnp.int32, (MLSTM_W, MLSTM_W), 0) >> 6
    cc = lax.broadcasted_iota(jnp.int32, (MLSTM_W, MLSTM_W), 1) >> 6
    c_ref[si] = decay_l * c_mat + jnp.where(rr == cc, kv, 0.0)
    n_ref[si] = decay_l * n_vec + jnp.sum(kw, axis=0, keepdims=True)
    m_ref[si] = _expand_heads(mnew_c)
    return h_out


def _mlstm_body(qf_ref, kf_ref, vf_ref, gf_ref, gtf_ref, qb_ref, kb_ref, vb_ref, gb_ref, gtb_ref,
                bd1_ref, hf_ref, hb_ref, c_ref, n_ref, m_ref):
    @pl.when(pl.program_id(1) == 0)
    def _():
        c_ref[...] = jnp.zeros_like(c_ref)
        n_ref[...] = jnp.zeros_like(n_ref)
        m_ref[...] = jnp.zeros_like(m_ref)

    bd1 = bd1_ref[...]
    hf_ref[0] = _mlstm_direction(0, qf_ref[0], kf_ref[0], vf_ref[0], gf_ref[0], gtf_ref[0],
                                 c_ref, n_ref, m_ref, bd1)
    hb_ref[0] = _mlstm_direction(1, qb_ref[0], kb_ref[0], vb_ref[0], gb_ref[0], gtb_ref[0],
                                 c_ref, n_ref, m_ref, bd1)


def _mlstm(mq, mk, mv, gate, gate_t, bd1):
    b, s, w = mq.shape
    L = MLSTM_CHUNK
    nc = s // L
    fwd = pl.BlockSpec((1, L, w), lambda bi, c: (bi, c, 0))
    bwd = pl.BlockSpec((1, L, w), lambda bi, c: (bi, nc - 1 - c, 0))
    gfwd = pl.BlockSpec((1, L, LANES), lambda bi, c: (bi, c, 0))
    gbwd = pl.BlockSpec((1, L, LANES), lambda bi, c: (bi, nc - 1 - c, 0))
    gtfwd = pl.BlockSpec((1, N_GATES, L), lambda bi, c: (bi, 0, c))
    gtbwd = pl.BlockSpec((1, N_GATES, L), lambda bi, c: (bi, 0, nc - 1 - c))
    n_scan = 2
    return pl.pallas_call(
        _mlstm_body,
        out_shape=(jax.ShapeDtypeStruct((b, s, w), F32), jax.ShapeDtypeStruct((b, s, w), F32)),
        grid=(b, nc),
        in_specs=[fwd, fwd, fwd, gfwd, gtfwd, bwd, bwd, bwd, gbwd, gtbwd, _const_spec(bd1.shape)],
        out_specs=(fwd, bwd),
        scratch_shapes=[pltpu.VMEM((n_scan, w, w), F32), pltpu.VMEM((n_scan, 1, w), F32),
                        pltpu.VMEM((n_scan, 1, w), F32)],
        compiler_params=_cparams(("parallel", "arbitrary")),
        name="mlstm_scan",
    )(mq, mk, mv, gate, gate_t, mq, mk, mv, gate, gate_t, bd1)


def _memkv_body(mem_ref, g_ref, w_ref, kg_ref, k_ref, v_ref):
    mb = _rms(mem_ref[0], g_ref[...]).astype(BF16)
    kv = jnp.dot(mb, w_ref[...], preferred_element_type=F32)
    d = kv.shape[1] // 2
    hd = d // XATTN_HEADS
    for h in range(XATTN_HEADS):
        k_ref[0, :, h * hd:(h + 1) * hd] = _rms(kv[:, h * hd:(h + 1) * hd], kg_ref[...]).astype(BF16)
    v_ref[0] = kv[:, d:].astype(BF16)


def _memkv(mem, g, wkv, layer, k_gain):
    b, m, d = mem.shape
    spec = pl.BlockSpec((1, m, d), lambda bi: (bi, 0, 0))
    return pl.pallas_call(
        _memkv_body,
        out_shape=(jax.ShapeDtypeStruct((b, m, d), BF16), jax.ShapeDtypeStruct((b, m, d), BF16)),
        grid=(b,),
        in_specs=[spec, _const_spec((1, d)), _layer_spec(wkv.shape[1:], layer), _const_spec(k_gain.shape)],
        out_specs=(spec, spec),
        compiler_params=_cparams(("parallel",)),
        name="memory_kv",
    )(mem, g, wkv, k_gain)


def _mix_out_body(x_ref, ya_ref, yc_ref, hf_ref, hb_ref, og_ref, mg_ref, bd_ref, wo_ref,
                  xg_ref, wq_ref, qg_ref, k_ref, v_ref, wxo_ref, o_ref):
    hs = hf_ref[0] + hb_ref[0]
    hn = hs * lax.rsqrt(_group_mean(hs * hs, bd_ref[...]) + EPS) * mg_ref[...]
    ym = (og_ref[0] * hn).astype(BF16)
    x = x_ref[0]
    x = (x + jnp.dot(ya_ref[0], wo_ref[0:ATTN_W, :], preferred_element_type=F32)
         + jnp.dot(yc_ref[0], wo_ref[ATTN_W:ATTN_W + CONV_CH, :], preferred_element_type=F32)
         + jnp.dot(ym, wo_ref[ATTN_W + CONV_CH:, :], preferred_element_type=F32))

    xb = _rms(x, xg_ref[...]).astype(BF16)
    qx = jnp.dot(xb, wq_ref[...], preferred_element_type=F32)
    hd = qx.shape[1] // XATTN_HEADS
    outs = []
    for h in range(XATTN_HEADS):
        sl = slice(h * hd, (h + 1) * hd)
        qh = (_rms(qx[:, sl], qg_ref[...]) * (hd ** -0.5)).astype(BF16)
        s = lax.dot_general(qh, k_ref[0, :, sl], (((1,), (1,)), ((), ())), preferred_element_type=F32)
        p = jnp.exp(s - jnp.max(s, axis=-1, keepdims=True))
        l = jnp.sum(p, axis=-1, keepdims=True)
        outs.append((jnp.dot(p.astype(BF16), v_ref[0, :, sl], preferred_element_type=F32) / l).astype(BF16))
    o = jnp.concatenate(outs, axis=1)
    o_ref[0] = x + jnp.dot(o, wxo_ref[...], preferred_element_type=F32)


def _mix_out(x, ya, yc, hf, hb, og, m_gain, bd, w_out, x_gain, wq, q_gain, mk, mv, wo, layer, *, tm):
    b, s, d = x.shape
    tok = lambda width: pl.BlockSpec((1, tm, width), lambda bi, i: (bi, i, 0))
    mem_spec = pl.BlockSpec((1,) + mk.shape[1:], lambda bi, i: (bi, 0, 0))
    return pl.pallas_call(
        _mix_out_body,
        out_shape=jax.ShapeDtypeStruct((b, s, d), F32),
        grid=(b, s // tm),
        in_specs=[tok(d), tok(ATTN_W), tok(CONV_CH), tok(MLSTM_W), tok(MLSTM_W), tok(MLSTM_W),
                  _const_spec(m_gain.shape), _const_spec(bd.shape), _layer_spec(w_out.shape[1:], layer),
                  _const_spec(x_gain.shape), _layer_spec(wq.shape[1:], layer), _const_spec(q_gain.shape),
                  mem_spec, mem_spec, _layer_spec(wo.shape[1:], layer)],
        out_specs=tok(d),
        compiler_params=_cparams(("parallel", "parallel")),
        name="mix_out_cross_attention",
    )(x, ya, yc, hf, hb, og, m_gain, bd, w_out, x_gain, wq, q_gain, mk, mv, wo)


def _rope_tables(s_len):
    rows = s_len // GRID_W
    row_idx = jnp.repeat(jnp.arange(rows, dtype=jnp.int32), GRID_W).astype(F32)
    col_idx = jnp.tile(jnp.arange(GRID_W, dtype=jnp.int32), rows).astype(F32)
    n_freq = HEAD_DIM // 4
    inv_freq = jnp.float32(ROPE_THETA) ** (-jnp.arange(n_freq, dtype=F32) / n_freq)
    ang = jnp.concatenate([row_idx[:, None] * inv_freq, col_idx[:, None] * inv_freq], axis=-1)
    cos = jnp.repeat(jnp.cos(ang), 2, axis=-1)
    sin = jnp.repeat(jnp.sin(ang), 2, axis=-1) * jnp.tile(jnp.array([-1.0, 1.0], F32), HEAD_DIM // 2)
    return jnp.tile(cos, (1, 2)), jnp.tile(sin, (1, 2))


def _block_diag(scale):
    idx = jnp.arange(MLSTM_W) // MLSTM_HEAD_DIM
    return jnp.where(idx[:, None] == idx[None, :], scale, 0.0).astype(BF16)


def _pick_tile(n, pref):
    t = min(n, pref)
    while n % t:
        t //= 2
    return t


def kernel(x, mem, ffn1_norm, ffn1_w13, ffn1_w2, mix_norm, w_in, attn_q_norm, attn_k_norm, conv_dw_w,
           conv_dw_b, conv_ln_g, conv_ln_b, mlstm_gate_b, mlstm_out_norm, w_out, xattn_norm, mem_norm,
           xattn_wq, xattn_wkv, xattn_q_norm, xattn_k_norm, xattn_wo, ffn2_norm, ffn2_w13, ffn2_w2):
    b, s, d = x.shape
    depth = w_in.shape[0]
    cos, sin = _rope_tables(s)
    bd_mean = _block_diag(1.0 / MLSTM_HEAD_DIM)
    bd_sum = _block_diag(1.0)

    tm_ffn = _pick_tile(b * s, 512)
    tm_tok = _pick_tile(s, 512)
    tq = _pick_tile(s, 512)
    ts = _pick_tile(s, 256)
    row = lambda v: v.reshape(1, -1)

    bf = lambda w: w.astype(BF16)
    ffn1_w13, ffn1_w2, ffn2_w13, ffn2_w2 = bf(ffn1_w13), bf(ffn1_w2), bf(ffn2_w13), bf(ffn2_w2)
    w_out, xattn_wq, xattn_wkv, xattn_wo = bf(w_out), bf(xattn_wq), bf(xattn_wkv), bf(xattn_wo)
    w_gate = w_in[:, :, OFF_MG:]
    w_gate_pad = bf(jnp.pad(w_gate, ((0, 0), (0, 0), (0, LANES - N_GATES))))
    w_gate_t = bf(jnp.swapaxes(w_gate, 1, 2))
    w_in = bf(w_in)

    for l in range(depth):
        x = _ffn(x.reshape(b * s, d), row(ffn1_norm[l]), ffn1_w13, ffn1_w2, l,
                 tm=tm_ffn, n_chunks=2).reshape(b, s, d)

        gate_b = mlstm_gate_b[l].reshape(1, N_GATES)
        q, kk, vt, u, mq, mk, mv, og, gate, gate_t = _inproj(
            x, row(mix_norm[l]), w_in, w_gate_pad, w_gate_t, l,
            jnp.pad(gate_b, ((0, 0), (0, LANES - N_GATES))), gate_b.reshape(N_GATES, 1),
            row(jnp.tile(attn_q_norm[l], ATTN_HEADS)), row(jnp.tile(attn_k_norm[l], ATTN_KV_HEADS)),
            cos, sin, bd_mean, tm=tm_tok)

        ya = _attention(q, kk, vt, tq=tq)
        yc = _conv(u, conv_dw_w[l], row(conv_dw_b[l]), row(conv_ln_g[l]), row(conv_ln_b[l]), ts=ts)
        hf, hb = _mlstm(mq, mk, mv, gate, gate_t, bd_sum)
        mem_k, mem_v = _memkv(mem, row(mem_norm[l]), xattn_wkv, l, row(xattn_k_norm[l]))
        x = _mix_out(x, ya, yc, hf, hb, og, row(mlstm_out_norm[l]), bd_mean, w_out,
                     row(xattn_norm[l]), xattn_wq, row(xattn_q_norm[l]), mem_k, mem_v, xattn_wo, l, tm=tm_tok)

        x = _ffn(x.reshape(b * s, d), row(ffn2_norm[l]), ffn2_w13, ffn2_w2, l,
                 tm=tm_ffn, n_chunks=2).reshape(b, s, d)
    return x
```

```python
import functools

import jax
import jax.numpy as jnp
from jax import lax
from jax.experimental import pallas as pl
from jax.experimental.pallas import tpu as pltpu

F32 = jnp.float32
BF16 = jnp.bfloat16

GRID_W = 64
EPS = 1e-6
LOG2E = 1.4426950408889634
ATTN_HEADS = 8
ATTN_KV_HEADS = 2
HEAD_DIM = 64
ATTN_W = ATTN_HEADS * HEAD_DIM
KV_W = ATTN_KV_HEADS * HEAD_DIM
ROPE_THETA = 10000.0
CONV_CH = 256
CONV_WIDTH = 31
CONV_PAD = CONV_WIDTH // 2
MLSTM_HEADS = 4
MLSTM_HEAD_DIM = 64
MLSTM_W = MLSTM_HEADS * MLSTM_HEAD_DIM
MLSTM_CHUNK = 128
N_GATES = 16
XATTN_HEADS = 4
OFF_K = ATTN_W
OFF_V = OFF_K + KV_W
OFF_CONV = OFF_V + KV_W
OFF_MQ = OFF_CONV + 2 * CONV_CH
OFF_MK = OFF_MQ + MLSTM_W
OFF_MV = OFF_MK + MLSTM_W
OFF_MO = OFF_MV + MLSTM_W
OFF_MG = OFF_MO + MLSTM_W

LANES = 128
SUBLANES = 8
HALO_ROWS = 16
KEY_CHUNK = 512
FFN_CHUNKS = 1
BF16_ROWS = 16
VT_ROWS = HEAD_DIM + BF16_ROWS
VMEM_LIMIT = 56 * 1024 * 1024


def _cparams(sem):
    return pltpu.CompilerParams(dimension_semantics=sem, vmem_limit_bytes=VMEM_LIMIT)


def _const_spec(shape):
    nd = len(shape)
    return pl.BlockSpec(shape, lambda *_: (0,) * nd)


def _layer_spec(block2d, layer, **kw):
    return pl.BlockSpec((None,) + tuple(block2d), lambda *_: (layer, 0, 0), **kw)


def _rms(x, g):
    return x * lax.rsqrt(jnp.mean(x * x, axis=-1, keepdims=True) + EPS) * g


def _group_mean(sq, bd):
    hi = sq.astype(BF16)
    lo = (sq - hi.astype(F32)).astype(BF16)
    return (jnp.dot(hi, bd, preferred_element_type=F32)
            + jnp.dot(lo, bd, preferred_element_type=F32))


def _swap_pairs(x):
    nxt = pltpu.roll(x, LANES - 1, axis=1)
    prv = pltpu.roll(x, 1, axis=1)
    lane = lax.broadcasted_iota(jnp.int32, x.shape, 1)
    return jnp.where((lane & 1) == 0, nxt, prv)


def _sigmoid(x):
    return 1.0 / (1.0 + jnp.exp(-x))


def _ffn_body(x_ref, g_ref, w13_ref, w2_ref, o_ref, *, d_ff, n_chunks):
    x = x_ref[...]
    xb = _rms(x, g_ref[...]).astype(BF16)
    fc = d_ff // n_chunks
    acc = None
    for c in range(n_chunks):
        a = jnp.dot(xb, w13_ref[:, c * fc:(c + 1) * fc], preferred_element_type=F32)
        b = jnp.dot(xb, w13_ref[:, d_ff + c * fc:d_ff + (c + 1) * fc], preferred_element_type=F32)
        act = (a * _sigmoid(a) * b).astype(BF16)
        part = jnp.dot(act, w2_ref[c * fc:(c + 1) * fc, :], preferred_element_type=F32)
        acc = part if acc is None else acc + part
    o_ref[...] = x + 0.5 * acc


def _ffn(x2d, g, w13, w2, layer, *, tm, n_chunks):
    n, d = x2d.shape
    d_ff = w2.shape[1]
    return pl.pallas_call(
        functools.partial(_ffn_body, d_ff=d_ff, n_chunks=n_chunks),
        out_shape=jax.ShapeDtypeStruct((n, d), F32),
        grid=(n // tm,),
        in_specs=[
            pl.BlockSpec((tm, d), lambda i: (i, 0)),
            _const_spec((1, d)),
            _layer_spec((d, 2 * d_ff), layer, pipeline_mode=pl.Buffered(1)),
            _layer_spec((d_ff, d), layer, pipeline_mode=pl.Buffered(1)),
        ],
        out_specs=pl.BlockSpec((tm, d), lambda i: (i, 0)),
        compiler_params=_cparams(("parallel",)),
        name="swiglu_half_step",
    )(x2d, g, w13, w2)


def _inproj_body(x_ref, g_ref, w_ref, wg_ref, wgt_ref, gb_ref, gbt_ref, qg_ref, kg_ref,
                 cos_ref, sin_ref, bd_ref,
                 q_ref, k_ref, vt_ref, u_ref, mq_ref, mk_ref, mv_ref, og_ref, gate_ref, gatet_ref):
    xb = _rms(x_ref[0], g_ref[...]).astype(BF16)
    h = jnp.dot(xb, w_ref[...], preferred_element_type=F32)
    bd = bd_ref[...]
    cos = cos_ref[...]
    sin = sin_ref[...]

    hq = h[:, 0:ATTN_W]
    ms = jnp.concatenate([_group_mean(hq[:, 0:256] * hq[:, 0:256], bd),
                          _group_mean(hq[:, 256:512] * hq[:, 256:512], bd)], axis=1)
    qn = hq * lax.rsqrt(ms + EPS) * qg_ref[...]
    in_first = lax.broadcasted_iota(jnp.int32, cos.shape, 1) < HEAD_DIM
    group = ATTN_HEADS // ATTN_KV_HEADS
    for j in range(ATTN_W // LANES):
        z = qn[:, j * LANES:(j + 1) * LANES]
        r = (z * cos + _swap_pairs(z) * sin) * (HEAD_DIM ** -0.5 * LOG2E)
        rolled = pltpu.roll(r, HEAD_DIM, axis=1)
        for half in range(2):
            hd = 2 * j + half
            g = hd // group
            z = r if half == g else rolled
            keep = jnp.where(in_first, z, 0.0) if g == 0 else jnp.where(in_first, 0.0, z)
            q_ref[0, hd] = keep.astype(BF16)

    hk = h[:, OFF_K:OFF_V]
    msk = _group_mean(hk * hk, bd_ref[0:LANES, 0:LANES])
    kn = hk * lax.rsqrt(msk + EPS) * kg_ref[...]
    k_ref[0] = (kn * cos + _swap_pairs(kn) * sin).astype(BF16)
    hv = h[:, OFF_V:OFF_CONV]
    for r in range(hv.shape[0] // KEY_CHUNK):
        vt = hv[r * KEY_CHUNK:(r + 1) * KEY_CHUNK, :].T.astype(BF16)
        for g in range(ATTN_KV_HEADS):
            vt_ref[0, r, g, 0:HEAD_DIM] = vt[g * HEAD_DIM:(g + 1) * HEAD_DIM, :]
            vt_ref[0, r, g, HEAD_DIM:VT_ROWS] = jnp.ones((VT_ROWS - HEAD_DIM, KEY_CHUNK), BF16)

    u_ref[0] = h[:, OFF_CONV:OFF_CONV + CONV_CH] * _sigmoid(h[:, OFF_CONV + CONV_CH:OFF_MQ])

    mq_ref[0] = h[:, OFF_MQ:OFF_MK].astype(BF16)
    mk_ref[0] = (h[:, OFF_MK:OFF_MV] * (MLSTM_HEAD_DIM ** -0.5)).astype(BF16)
    mv_ref[0] = h[:, OFF_MV:OFF_MO].astype(BF16)
    og_ref[0] = _sigmoid(h[:, OFF_MO:OFF_MG])

    def log_gate(pre, is_forget):
        ls = jnp.minimum(pre, 0.0) - jnp.log1p(jnp.exp(-jnp.abs(pre)))
        return jnp.where(is_forget, ls, pre)

    gp = jnp.dot(xb, wg_ref[...], preferred_element_type=F32) + gb_ref[...]
    col = lax.broadcasted_iota(jnp.int32, gp.shape, 1)
    gate_ref[0] = log_gate(gp, ((col >> 2) & 1) == 1)
    gpt = lax.dot_general(wgt_ref[...], xb, (((1,), (1,)), ((), ())),
                          preferred_element_type=F32) + gbt_ref[...]
    row = lax.broadcasted_iota(jnp.int32, gpt.shape, 0)
    gatet_ref[0] = log_gate(gpt, ((row >> 2) & 1) == 1)


def _inproj(x, g, w_in, w_gate, w_gate_t, layer, gate_b, gate_bt, q_gain, k_gain, cos, sin, bd, *, tm):
    b, s, d = x.shape
    tok = lambda width, dt: jax.ShapeDtypeStruct((b, s, width), dt)
    tok_spec = lambda width: pl.BlockSpec((1, tm, width), lambda bi, i: (bi, i, 0))
    q_spec = pl.BlockSpec((1, ATTN_HEADS, tm, LANES), lambda bi, i: (bi, 0, i, 0))
    vt_shape = (b, s // KEY_CHUNK, ATTN_KV_HEADS, VT_ROWS, KEY_CHUNK)
    vt_spec = pl.BlockSpec((1, tm // KEY_CHUNK) + vt_shape[2:], lambda bi, i: (bi, i, 0, 0, 0))
    return pl.pallas_call(
        _inproj_body,
        out_shape=(
            jax.ShapeDtypeStruct((b, ATTN_HEADS, s, LANES), BF16),
            tok(KV_W, BF16),
            jax.ShapeDtypeStruct(vt_shape, BF16),
            tok(CONV_CH, F32),
            tok(MLSTM_W, BF16), tok(MLSTM_W, BF16), tok(MLSTM_W, BF16),
            tok(MLSTM_W, F32),
            tok(LANES, F32),
            jax.ShapeDtypeStruct((b, N_GATES, s), F32),
        ),
        grid=(b, s // tm),
        in_specs=[
            tok_spec(d),
            _const_spec((1, d)),
            _layer_spec((d, OFF_MG), layer),
            _layer_spec((d, LANES), layer),
            _layer_spec((N_GATES, d), layer),
            _const_spec((1, LANES)),
            _const_spec((N_GATES, 1)),
            _const_spec((1, ATTN_W)),
            _const_spec((1, LANES)),
            pl.BlockSpec((tm, LANES), lambda bi, i: (i, 0)),
            pl.BlockSpec((tm, LANES), lambda bi, i: (i, 0)),
            _const_spec(bd.shape),
        ],
        out_specs=(
            q_spec, tok_spec(KV_W), vt_spec, tok_spec(CONV_CH),
            tok_spec(MLSTM_W), tok_spec(MLSTM_W), tok_spec(MLSTM_W), tok_spec(MLSTM_W),
            tok_spec(LANES),
            pl.BlockSpec((1, N_GATES, tm), lambda bi, i: (bi, 0, i)),
        ),
        compiler_params=_cparams(("parallel", "parallel")),
        name="input_projection",
    )(x, g, w_in, w_gate, w_gate_t, gate_b, gate_bt, q_gain, k_gain, cos, sin, bd)


def _attn_body(q_ref, k_ref, vt_ref, o_ref, s_ref, mx_ref, *, unroll):
    tq = q_ref.shape[2]
    n_chunks = vt_ref.shape[1]
    tk = vt_ref.shape[4]
    nt = (((1,), (1,)), ((), ()))

    def scores(c, slot):
        kc = k_ref[0, pl.ds(pl.multiple_of(c * tk, tk), tk), :]
        for j in range(2):
            s = lax.dot_general(kc, q_ref[0, j], nt, preferred_element_type=F32)
            s_ref[slot, j] = s
            mx_ref[slot, j] = jnp.max(s, axis=0, keepdims=True)

    def consume(c, slot, state):
        vtc = vt_ref[0, c, 0]
        out = []
        for j in range(2):
            m, acc = state[j]
            s = s_ref[slot, j]
            m_new = jnp.maximum(m, mx_ref[slot, j])
            alpha = jnp.exp2(m - m_new)
            p = jnp.exp2(s - m_new).astype(BF16)
            out.append((m_new, alpha * acc + jnp.dot(vtc, p, preferred_element_type=F32)))
        return tuple(out)

    def step(i, state):
        for u in range(unroll):
            c = unroll * i + u
            scores(jnp.minimum(c + 1, n_chunks - 1), (u + 1) % 2)
            state = consume(c, u % 2, state)
        return state

    init = tuple((jnp.full((1, tq), -jnp.inf, F32), jnp.zeros((VT_ROWS, tq), F32)) for _ in range(2))
    scores(0, 0)
    (_, a0), (_, a1) = lax.fori_loop(0, n_chunks // unroll, step, init)
    o = [a[0:HEAD_DIM] / a[HEAD_DIM:HEAD_DIM + 1] for a in (a0, a1)]
    o_ref[0] = jnp.concatenate(o, axis=0).T.astype(BF16)


def _attention(q, k, vt, *, tq):
    b, _, s, _ = q.shape
    n_chunks, _, hd, tk = vt.shape[1:]
    unroll = max(u for u in (8, 4, 2) if n_chunks % u == 0)
    assert n_chunks % unroll == 0
    heads_per_kv = ATTN_HEADS // ATTN_KV_HEADS
    return pl.pallas_call(
        functools.partial(_attn_body, unroll=unroll),
        out_shape=jax.ShapeDtypeStruct((b, s, ATTN_W), BF16),
        grid=(b, ATTN_HEADS // 2, s // tq),
        in_specs=[pl.BlockSpec((1, 2, tq, LANES), lambda bi, p, i: (bi, p, i, 0)),
                  pl.BlockSpec((1, s, KV_W), lambda bi, p, i: (bi, 0, 0)),
                  pl.BlockSpec((1, n_chunks, 1, hd, tk), lambda bi, p, i: (bi, 0, (2 * p) // heads_per_kv, 0, 0))],
        out_specs=pl.BlockSpec((1, tq, LANES), lambda bi, p, i: (bi, i, p)),
        scratch_shapes=[pltpu.VMEM((2, 2, tk, tq), F32), pltpu.VMEM((2, 2, 1, tq), F32)],
        compiler_params=_cparams(("parallel", "parallel", "arbitrary")),
        name="gqa_attention",
    )(q, k, vt)


def _conv_body(prev_ref, cur_ref, next_ref, w_ref, b_ref, lg_ref, lb_ref, o_ref, shift_ref):
    i = pl.program_id(1)
    ts = cur_ref.shape[1]
    prev = jnp.where(i > 0, prev_ref[0], 0.0)
    nxt = jnp.where(i < pl.num_programs(1) - 1, next_ref[0], 0.0)
    win = jnp.concatenate([prev, cur_ref[0], nxt], axis=0)
    w = w_ref[...]
    acc = jnp.zeros((ts, CONV_CH), F32) + b_ref[...]
    first_off = HALO_ROWS - CONV_PAD
    span = shift_ref.shape[0] - ts
    assert span >= ((first_off + CONV_WIDTH - 1) // SUBLANES) * SUBLANES
    assert SUBLANES - 1 + ts + span <= win.shape[0]
    for r in range(SUBLANES):
        shift_ref[...] = win[r:r + ts + span, :]
        for t in range(CONV_WIDTH):
            off = first_off + t
            if off % SUBLANES == r:
                a = off - r
                acc = acc + shift_ref[a:a + ts, :] * w[t:t + 1, :]
    mu = jnp.mean(acc, axis=-1, keepdims=True)
    cen = acc - mu
    var = jnp.mean(cen * cen, axis=-1, keepdims=True)
    un = cen * lax.rsqrt(var + EPS) * lg_ref[...] + lb_ref[...]
    o_ref[0] = (un * _sigmoid(un)).astype(BF16)


def _conv(u, dw_w, dw_b, ln_g, ln_b, *, ts):
    b, s, c = u.shape
    r = ts // HALO_ROWS
    n_halo = s // HALO_ROWS
    return pl.pallas_call(
        _conv_body,
        out_shape=jax.ShapeDtypeStruct((b, s, c), BF16),
        grid=(b, s // ts),
        in_specs=[
            pl.BlockSpec((1, HALO_ROWS, c), lambda bi, i: (bi, jnp.maximum(i * r - 1, 0), 0)),
            pl.BlockSpec((1, ts, c), lambda bi, i: (bi, i, 0)),
            pl.BlockSpec((1, HALO_ROWS, c), lambda bi, i: (bi, jnp.minimum((i + 1) * r, n_halo - 1), 0)),
            _const_spec((CONV_WIDTH, c)),
            _const_spec((1, c)), _const_spec((1, c)), _const_spec((1, c)),
        ],
        out_specs=pl.BlockSpec((1, ts, c), lambda bi, i: (bi, i, 0)),
        scratch_shapes=[pltpu.VMEM((ts + 2 * HALO_ROWS - SUBLANES, c), F32)],
        compiler_params=_cparams(("parallel", "parallel")),
        name="conformer_conv",
    )(u, u, u, dw_w, dw_b, ln_g, ln_b)


def _expand_heads(cols):
    rows = cols[0].shape[0]
    hid = lax.broadcasted_iota(jnp.int32, (rows, MLSTM_W), 1) >> 6
    out = jnp.broadcast_to(cols[3], (rows, MLSTM_W))
    for h in (2, 1, 0):
        out = jnp.where(hid == h, cols[h], out)
    return out


def _split3(x):
    a = x.astype(BF16)
    r = x - a.astype(F32)
    b = r.astype(BF16)
    c = (r - b.astype(F32)).astype(BF16)
    return a, b, c


def _mlstm_direction(d, q, k, v, gate, gate_t, c_ref, n_ref, m_ref, bd1):
    si = d
    L = q.shape[0]
    r_i = lax.broadcasted_iota(jnp.int32, (L, L), 0)
    c_i = lax.broadcasted_iota(jnp.int32, (L, L), 1)
    keep = (c_i <= r_i) if d == 0 else (c_i >= r_i)
    tri = jnp.where(keep, 1.0, 0.0).astype(BF16)
    tri_t = jnp.where((r_i <= c_i) if d == 0 else (r_i >= c_i), 1.0, 0.0).astype(BF16)

    cum_col = sum(jnp.dot(tri, t, preferred_element_type=F32) for t in _split3(gate))
    cum_row = sum(jnp.dot(t, tri_t, preferred_element_type=F32) for t in _split3(gate_t))
    tot = jnp.sum(gate, axis=0, keepdims=True)

    hid = lax.broadcasted_iota(jnp.int32, (L, MLSTM_W), 1) >> 6
    m_prev_l = m_ref[si]
    qf = q.astype(F32)
    kf = k.astype(F32)

    m_t_c, inter_c, rs_c, wend_c, mnew_c, decay_c = [], [], [], [], [], []
    num_intra = jnp.zeros((L, MLSTM_W), F32)
    for h in range(MLSTM_HEADS):
        ji = d * 8 + h
        jf = d * 8 + MLSTM_HEADS + h
        m_prev = m_prev_l[:, h * MLSTM_HEAD_DIM:h * MLSTM_HEAD_DIM + 1]
        b_col = cum_col[:, jf:jf + 1]
        i_col = gate[:, ji:ji + 1]
        b_row = cum_row[jf:jf + 1, :]
        i_row = gate_t[ji:ji + 1, :]
        b_end = tot[:, jf:jf + 1]

        logw = jnp.where(keep, b_col - b_row + i_row, -jnp.inf)
        m_inter = b_col + m_prev
        m_t = jnp.maximum(jnp.max(logw, axis=-1, keepdims=True), m_inter)
        w = jnp.exp(logw - m_t)
        qm = jnp.where(hid == h, qf, 0.0).astype(BF16)
        s = lax.dot_general(qm, k, (((1,), (1,)), ((), ())), preferred_element_type=F32) * w
        r = jnp.dot(s.astype(BF16), v, preferred_element_type=F32)
        num_intra = jnp.where(hid == h, r, num_intra)
        m_t_c.append(m_t)
        inter_c.append(jnp.exp(m_inter - m_t))
        rs_c.append(jnp.sum(s, axis=-1, keepdims=True))

        logw_end = b_end - b_col + i_col
        m_new = jnp.maximum(b_end + m_prev, jnp.max(logw_end, axis=0, keepdims=True))
        wend_c.append(jnp.exp(logw_end - m_new))
        mnew_c.append(m_new)
        decay_c.append(jnp.exp(b_end + m_prev - m_new))

    m_t_l = _expand_heads(m_t_c)
    inter_l = _expand_heads(inter_c)
    c_mat = c_ref[si]
    n_vec = n_ref[si]
    c_hi = c_mat.astype(BF16)
    c_lo = (c_mat - c_hi.astype(F32)).astype(BF16)
    qc = (jnp.dot(q, c_hi, preferred_element_type=F32)
          + jnp.dot(q, c_lo, preferred_element_type=F32))
    qn_l = _group_mean(qf * n_vec, bd1)
    num = num_intra + inter_l * qc
    den = _expand_heads(rs_c) + inter_l * qn_l
    h_out = num / jnp.maximum(jnp.abs(den), jnp.exp(-m_t_l))

    decay_l = _expand_heads(decay_c)
    kw = kf * _expand_heads(wend_c)
    kv = lax.dot_general(kw.astype(BF16), v, (((0,), (0,)), ((), ())),
                         preferred_element_type=F32)
    rr = lax.broadcasted_iota(jnp.int32, (MLSTM_W, MLSTM_W), 0) >> 6
    cc = lax.broadcasted_iota(jnp.int32, (MLSTM_W, MLSTM_W), 1) >> 6
    c_ref[si] = decay_l * c_mat + jnp.where(rr == cc, kv, 0.0)
    n_ref[si] = decay_l * n_vec + jnp.sum(kw, axis=0, keepdims=True)
    m_ref[si] = _expand_heads(mnew_c)
    return h_out


def _mlstm_body(qf_ref, kf_ref, vf_ref, gf_ref, gtf_ref, qb_ref, kb_ref, vb_ref, gb_ref, gtb_ref,
                bd1_ref, hf_ref, hb_ref, c_ref, n_ref, m_ref):
    @pl.when(pl.program_id(1) == 0)
    def _():
        c_ref[...] = jnp.zeros_like(c_ref)
        n_ref[...] = jnp.zeros_like(n_ref)
        m_ref[...] = jnp.zeros_like(m_ref)

    bd1 = bd1_ref[...]
    hf_ref[0] = _mlstm_direction(0, qf_ref[0], kf_ref[0], vf_ref[0], gf_ref[0], gtf_ref[0],
                                 c_ref, n_ref, m_ref, bd1)
    hb_ref[0] = _mlstm_direction(1, qb_ref[0], kb_ref[0], vb_ref[0], gb_ref[0], gtb_ref[0],
                                 c_ref, n_ref, m_ref, bd1)


def _mlstm(mq, mk, mv, gate, gate_t, bd1):
    b, s, w = mq.shape
    L = MLSTM_CHUNK
    nc = s // L
    fwd = pl.BlockSpec((1, L, w), lambda bi, c: (bi, c, 0))
    bwd = pl.BlockSpec((1, L, w), lambda bi, c: (bi, nc - 1 - c, 0))
    gfwd = pl.BlockSpec((1, L, LANES), lambda bi, c: (bi, c, 0))
    gbwd = pl.BlockSpec((1, L, LANES), lambda bi, c: (bi, nc - 1 - c, 0))
    gtfwd = pl.BlockSpec((1, N_GATES, L), lambda bi, c: (bi, 0, c))
    gtbwd = pl.BlockSpec((1, N_GATES, L), lambda bi, c: (bi, 0, nc - 1 - c))
    n_scan = 2
    return pl.pallas_call(
        _mlstm_body,
        out_shape=(jax.ShapeDtypeStruct((b, s, w), F32), jax.ShapeDtypeStruct((b, s, w), F32)),
        grid=(b, nc),
        in_specs=[fwd, fwd, fwd, gfwd, gtfwd, bwd, bwd, bwd, gbwd, gtbwd, _const_spec(bd1.shape)],
        out_specs=(fwd, bwd),
        scratch_shapes=[pltpu.VMEM((n_scan, w, w), F32), pltpu.VMEM((n_scan, 1, w), F32),
                        pltpu.VMEM((n_scan, 1, w), F32)],
        compiler_params=_cparams(("parallel", "arbitrary")),
        name="mlstm_scan",
    )(mq, mk, mv, gate, gate_t, mq, mk, mv, gate, gate_t, bd1)


def _memkv_body(mem_ref, g_ref, w_ref, kg_ref, k_ref, v_ref):
    mb = _rms(mem_ref[0], g_ref[...]).astype(BF16)
    kv = jnp.dot(mb, w_ref[...], preferred_element_type=F32)
    d = kv.shape[1] // 2
    hd = d // XATTN_HEADS
    for h in range(XATTN_HEADS):
        k_ref[0, :, h * hd:(h + 1) * hd] = _rms(kv[:, h * hd:(h + 1) * hd], kg_ref[...]).astype(BF16)
    v_ref[0] = kv[:, d:].astype(BF16)


def _memkv(mem, g, wkv, layer, k_gain):
    b, m, d = mem.shape
    spec = pl.BlockSpec((1, m, d), lambda bi: (bi, 0, 0))
    return pl.pallas_call(
        _memkv_body,
        out_shape=(jax.ShapeDtypeStruct((b, m, d), BF16), jax.ShapeDtypeStruct((b, m, d), BF16)),
        grid=(b,),
        in_specs=[spec, _const_spec((1, d)), _layer_spec(wkv.shape[1:], layer), _const_spec(k_gain.shape)],
        out_specs=(spec, spec),
        compiler_params=_cparams(("parallel",)),
        name="memory_kv",
    )(mem, g, wkv, k_gain)


def _mix_out_body(x_ref, ya_ref, yc_ref, hf_ref, hb_ref, og_ref, mg_ref, bd_ref, wo_ref,
                  xg_ref, wq_ref, qg_ref, k_ref, v_ref, wxo_ref, o_ref):
    hs = hf_ref[0] + hb_ref[0]
    hn = hs * lax.rsqrt(_group_mean(hs * hs, bd_ref[...]) + EPS) * mg_ref[...]
    ym = (og_ref[0] * hn).astype(BF16)
    x = x_ref[0]
    x = (x + jnp.dot(ya_ref[0], wo_ref[0:ATTN_W, :], preferred_element_type=F32)
         + jnp.dot(yc_ref[0], wo_ref[ATTN_W:ATTN_W + CONV_CH, :], preferred_element_type=F32)
         + jnp.dot(ym, wo_ref[ATTN_W + CONV_CH:, :], preferred_element_type=F32))

    xb = _rms(x, xg_ref[...]).astype(BF16)
    qx = jnp.dot(xb, wq_ref[...], preferred_element_type=F32)
    hd = qx.shape[1] // XATTN_HEADS
    outs = []
    for h in range(XATTN_HEADS):
        sl = slice(h * hd, (h + 1) * hd)
        qh = (_rms(qx[:, sl], qg_ref[...]) * (hd ** -0.5)).astype(BF16)
        s = lax.dot_general(qh, k_ref[0, :, sl], (((1,), (1,)), ((), ())), preferred_element_type=F32)
        p = jnp.exp(s - jnp.max(s, axis=-1, keepdims=True))
        l = jnp.sum(p, axis=-1, keepdims=True)
        outs.append((jnp.dot(p.astype(BF16), v_ref[0, :, sl], preferred_element_type=F32) / l).astype(BF16))
    o = jnp.concatenate(outs, axis=1)
    o_ref[0] = x + jnp.dot(o, wxo_ref[...], preferred_element_type=F32)


def _mix_out(x, ya, yc, hf, hb, og, m_gain, bd, w_out, x_gain, wq, q_gain, mk, mv, wo, layer, *, tm):
    b, s, d = x.shape
    tok = lambda width: pl.BlockSpec((1, tm, width), lambda bi, i: (bi, i, 0))
    mem_spec = pl.BlockSpec((1,) + mk.shape[1:], lambda bi, i: (bi, 0, 0))
    return pl.pallas_call(
        _mix_out_body,
        out_shape=jax.ShapeDtypeStruct((b, s, d), F32),
        grid=(b, s // tm),
        in_specs=[tok(d), tok(ATTN_W), tok(CONV_CH), tok(MLSTM_W), tok(MLSTM_W), tok(MLSTM_W),
                  _const_spec(m_gain.shape), _const_spec(bd.shape), _layer_spec(w_out.shape[1:], layer),
                  _const_spec(x_gain.shape), _layer_spec(wq.shape[1:], layer), _const_spec(q_gain.shape),
                  mem_spec, mem_spec, _layer_spec(wo.shape[1:], layer)],
        out_specs=tok(d),
        compiler_params=_cparams(("parallel", "parallel")),
        name="mix_out_cross_attention",
    )(x, ya, yc, hf, hb, og, m_gain, bd, w_out, x_gain, wq, q_gain, mk, mv, wo)


def _rope_tables(s_len):
    rows = s_len // GRID_W
    row_idx = jnp.repeat(jnp.arange(rows, dtype=jnp.int32), GRID_W).astype(F32)
    col_idx = jnp.tile(jnp.arange(GRID_W, dtype=jnp.int32), rows).astype(F32)
    n_freq = HEAD_DIM // 4
    inv_freq = jnp.float32(ROPE_THETA) ** (-jnp.arange(n_freq, dtype=F32) / n_freq)
    ang = jnp.concatenate([row_idx[:, None] * inv_freq, col_idx[:, None] * inv_freq], axis=-1)
    cos = jnp.repeat(jnp.cos(ang), 2, axis=-1)
    sin = jnp.repeat(jnp.sin(ang), 2, axis=-1) * jnp.tile(jnp.array([-1.0, 1.0], F32), HEAD_DIM // 2)
    return jnp.tile(cos, (1, 2)), jnp.tile(sin, (1, 2))


def _block_diag(scale):
    idx = jnp.arange(MLSTM_W) // MLSTM_HEAD_DIM
    return jnp.where(idx[:, None] == idx[None, :], scale, 0.0).astype(BF16)


def _pick_tile(n, pref):
    t = min(n, pref)
    while n % t:
        t //= 2
    return t


def kernel(x, mem, ffn1_norm, ffn1_w13, ffn1_w2, mix_norm, w_in, attn_q_norm, attn_k_norm, conv_dw_w,
           conv_dw_b, conv_ln_g, conv_ln_b, mlstm_gate_b, mlstm_out_norm, w_out, xattn_norm, mem_norm,
           xattn_wq, xattn_wkv, xattn_q_norm, xattn_k_norm, xattn_wo, ffn2_norm, ffn2_w13, ffn2_w2):
    b, s, d = x.shape
    depth = w_in.shape[0]
    cos, sin = _rope_tables(s)
    bd_mean = _block_diag(1.0 / MLSTM_HEAD_DIM)
    bd_sum = _block_diag(1.0)

    tm_ffn = _pick_tile(b * s, 512)
    tm_tok = _pick_tile(s, 512)
    tq = _pick_tile(s, 512)
    ts = _pick_tile(s, 256)
    row = lambda v: v.reshape(1, -1)

    bf = lambda w: w.astype(BF16)
    ffn1_w13, ffn1_w2, ffn2_w13, ffn2_w2 = bf(ffn1_w13), bf(ffn1_w2), bf(ffn2_w13), bf(ffn2_w2)
    w_out, xattn_wq, xattn_wkv, xattn_wo = bf(w_out), bf(xattn_wq), bf(xattn_wkv), bf(xattn_wo)
    w_gate = w_in[:, :, OFF_MG:]
    w_gate_pad = bf(jnp.pad(w_gate, ((0, 0), (0, 0), (0, LANES - N_GATES))))
    w_gate_t = bf(jnp.swapaxes(w_gate, 1, 2))
    w_in = bf(w_in)

    for l in range(depth):
        x = _ffn(x.reshape(b * s, d), row(ffn1_norm[l]), ffn1_w13, ffn1_w2, l,
                 tm=tm_ffn, n_chunks=FFN_CHUNKS).reshape(b, s, d)

        gate_b = mlstm_gate_b[l].reshape(1, N_GATES)
        q, kk, vt, u, mq, mk, mv, og, gate, gate_t = _inproj(
            x, row(mix_norm[l]), w_in, w_gate_pad, w_gate_t, l,
            jnp.pad(gate_b, ((0, 0), (0, LANES - N_GATES))), gate_b.reshape(N_GATES, 1),
            row(jnp.tile(attn_q_norm[l], ATTN_HEADS)), row(jnp.tile(attn_k_norm[l], ATTN_KV_HEADS)),
            cos, sin, bd_mean, tm=tm_tok)

        ya = _attention(q, kk, vt, tq=tq)
        yc = _conv(u, conv_dw_w[l], row(conv_dw_b[l]), row(conv_ln_g[l]), row(conv_ln_b[l]), ts=ts)
        hf, hb = _mlstm(mq, mk, mv, gate, gate_t, bd_sum)
        mem_k, mem_v = _memkv(mem, row(mem_norm[l]), xattn_wkv, l, row(xattn_k_norm[l]))
        x = _mix_out(x, ya, yc, hf, hb, og, row(mlstm_out_norm[l]), bd_mean, w_out,
                     row(xattn_norm[l]), xattn_wq, row(xattn_q_norm[l]), mem_k, mem_v, xattn_wo, l, tm=tm_tok)

        x = _ffn(x.reshape(b * s, d), row(ffn2_norm[l]), ffn2_w13, ffn2_w2, l,
                 tm=tm_ffn, n_chunks=FFN_CHUNKS).reshape(b, s, d)
    return x
```

```python
import functools

import jax
import jax.numpy as jnp
from jax import lax
from jax.experimental import pallas as pl
from jax.experimental.pallas import tpu as pltpu

F32 = jnp.float32
BF16 = jnp.bfloat16

GRID_W = 64
EPS = 1e-6
LOG2E = 1.4426950408889634
ATTN_HEADS = 8
ATTN_KV_HEADS = 2
HEAD_DIM = 64
ATTN_W = ATTN_HEADS * HEAD_DIM
KV_W = ATTN_KV_HEADS * HEAD_DIM
ROPE_THETA = 10000.0
CONV_CH = 256
CONV_WIDTH = 31
CONV_PAD = CONV_WIDTH // 2
MLSTM_HEADS = 4
MLSTM_HEAD_DIM = 64
MLSTM_W = MLSTM_HEADS * MLSTM_HEAD_DIM
MLSTM_CHUNK = 128
N_GATES = 16
XATTN_HEADS = 4
OFF_K = ATTN_W
OFF_V = OFF_K + KV_W
OFF_CONV = OFF_V + KV_W
OFF_MQ = OFF_CONV + 2 * CONV_CH
OFF_MK = OFF_MQ + MLSTM_W
OFF_MV = OFF_MK + MLSTM_W
OFF_MO = OFF_MV + MLSTM_W
OFF_MG = OFF_MO + MLSTM_W

LANES = 128
SUBLANES = 8
HALO_ROWS = 16
KEY_CHUNK = 512
FFN_CHUNKS = 1
BF16_ROWS = 16
VT_ROWS = HEAD_DIM + BF16_ROWS
VMEM_LIMIT = 56 * 1024 * 1024


def _cparams(sem):
    return pltpu.CompilerParams(dimension_semantics=sem, vmem_limit_bytes=VMEM_LIMIT)


def _const_spec(shape):
    nd = len(shape)
    return pl.BlockSpec(shape, lambda *_: (0,) * nd)


def _layer_spec(block2d, layer, **kw):
    return pl.BlockSpec((None,) + tuple(block2d), lambda *_: (layer, 0, 0), **kw)


def _rms(x, g):
    return x * lax.rsqrt(jnp.mean(x * x, axis=-1, keepdims=True) + EPS) * g


def _group_mean(sq, bd):
    hi = sq.astype(BF16)
    lo = (sq - hi.astype(F32)).astype(BF16)
    return (jnp.dot(hi, bd, preferred_element_type=F32)
            + jnp.dot(lo, bd, preferred_element_type=F32))


def _swap_pairs(x):
    nxt = pltpu.roll(x, LANES - 1, axis=1)
    prv = pltpu.roll(x, 1, axis=1)
    lane = lax.broadcasted_iota(jnp.int32, x.shape, 1)
    return jnp.where((lane & 1) == 0, nxt, prv)


def _sigmoid(x):
    return 1.0 / (1.0 + jnp.exp(-x))


def _ffn_body(x_ref, g_ref, w13_ref, w2_ref, o_ref, *, d_ff, n_chunks):
    x = x_ref[...]
    xb = _rms(x, g_ref[...]).astype(BF16)
    fc = d_ff // n_chunks
    acc = None
    for c in range(n_chunks):
        a = jnp.dot(xb, w13_ref[:, c * fc:(c + 1) * fc], preferred_element_type=F32)
        b = jnp.dot(xb, w13_ref[:, d_ff + c * fc:d_ff + (c + 1) * fc], preferred_element_type=F32)
        act = (a * _sigmoid(a) * b).astype(BF16)
        part = jnp.dot(act, w2_ref[c * fc:(c + 1) * fc, :], preferred_element_type=F32)
        acc = part if acc is None else acc + part
    o_ref[...] = x + 0.5 * acc


def _ffn(x2d, g, w13, w2, layer, *, tm, n_chunks):
    n, d = x2d.shape
    d_ff = w2.shape[1]
    return pl.pallas_call(
        functools.partial(_ffn_body, d_ff=d_ff, n_chunks=n_chunks),
        out_shape=jax.ShapeDtypeStruct((n, d), F32),
        grid=(n // tm,),
        in_specs=[
            pl.BlockSpec((tm, d), lambda i: (i, 0)),
            _const_spec((1, d)),
            _layer_spec((d, 2 * d_ff), layer, pipeline_mode=pl.Buffered(1)),
            _layer_spec((d_ff, d), layer, pipeline_mode=pl.Buffered(1)),
        ],
        out_specs=pl.BlockSpec((tm, d), lambda i: (i, 0)),
        compiler_params=_cparams(("parallel",)),
        name="swiglu_half_step",
    )(x2d, g, w13, w2)


def _inproj_body(x_ref, g_ref, w_ref, wg_ref, wgt_ref, gb_ref, gbt_ref, qg_ref, kg_ref,
                 cos_ref, sin_ref, bd_ref,
                 q_ref, k_ref, vt_ref, u_ref, mq_ref, mk_ref, mv_ref, og_ref, gate_ref, gatet_ref):
    xb = _rms(x_ref[0], g_ref[...]).astype(BF16)
    h = jnp.dot(xb, w_ref[...], preferred_element_type=F32)
    bd = bd_ref[...]
    cos = cos_ref[...]
    sin = sin_ref[...]

    hq = h[:, 0:ATTN_W]
    ms = jnp.concatenate([_group_mean(hq[:, 0:256] * hq[:, 0:256], bd),
                          _group_mean(hq[:, 256:512] * hq[:, 256:512], bd)], axis=1)
    qn = hq * lax.rsqrt(ms + EPS) * qg_ref[...]
    in_first = lax.broadcasted_iota(jnp.int32, cos.shape, 1) < HEAD_DIM
    group = ATTN_HEADS // ATTN_KV_HEADS
    for j in range(ATTN_W // LANES):
        z = qn[:, j * LANES:(j + 1) * LANES]
        r = (z * cos + _swap_pairs(z) * sin) * (HEAD_DIM ** -0.5 * LOG2E)
        rolled = pltpu.roll(r, HEAD_DIM, axis=1)
        for half in range(2):
            hd = 2 * j + half
            g = hd // group
            z = r if half == g else rolled
            keep = jnp.where(in_first, z, 0.0) if g == 0 else jnp.where(in_first, 0.0, z)
            q_ref[0, hd] = keep.astype(BF16)

    hk = h[:, OFF_K:OFF_V]
    msk = _group_mean(hk * hk, bd_ref[0:LANES, 0:LANES])
    kn = hk * lax.rsqrt(msk + EPS) * kg_ref[...]
    k_ref[0] = (kn * cos + _swap_pairs(kn) * sin).astype(BF16)
    hv = h[:, OFF_V:OFF_CONV]
    for r in range(hv.shape[0] // KEY_CHUNK):
        vt = hv[r * KEY_CHUNK:(r + 1) * KEY_CHUNK, :].T.astype(BF16)
        for g in range(ATTN_KV_HEADS):
            vt_ref[0, r, g, 0:HEAD_DIM] = vt[g * HEAD_DIM:(g + 1) * HEAD_DIM, :]
            vt_ref[0, r, g, HEAD_DIM:VT_ROWS] = jnp.ones((VT_ROWS - HEAD_DIM, KEY_CHUNK), BF16)

    u_ref[0] = h[:, OFF_CONV:OFF_CONV + CONV_CH] * _sigmoid(h[:, OFF_CONV + CONV_CH:OFF_MQ])

    mq_ref[0] = h[:, OFF_MQ:OFF_MK].astype(BF16)
    mk_ref[0] = (h[:, OFF_MK:OFF_MV] * (MLSTM_HEAD_DIM ** -0.5)).astype(BF16)
    mv_ref[0] = h[:, OFF_MV:OFF_MO].astype(BF16)
    og_ref[0] = _sigmoid(h[:, OFF_MO:OFF_MG])

    def log_gate(pre, is_forget):
        ls = jnp.minimum(pre, 0.0) - jnp.log1p(jnp.exp(-jnp.abs(pre)))
        return jnp.where(is_forget, ls, pre)

    gp = jnp.dot(xb, wg_ref[...], preferred_element_type=F32) + gb_ref[...]
    col = lax.broadcasted_iota(jnp.int32, gp.shape, 1)
    gate_ref[0] = log_gate(gp, ((col >> 2) & 1) == 1)
    gpt = lax.dot_general(wgt_ref[...], xb, (((1,), (1,)), ((), ())),
                          preferred_element_type=F32) + gbt_ref[...]
    row = lax.broadcasted_iota(jnp.int32, gpt.shape, 0)
    gatet_ref[0] = log_gate(gpt, ((row >> 2) & 1) == 1)


def _inproj(x, g, w_in, w_gate, w_gate_t, layer, gate_b, gate_bt, q_gain, k_gain, cos, sin, bd, *, tm):
    b, s, d = x.shape
    tok = lambda width, dt: jax.ShapeDtypeStruct((b, s, width), dt)
    tok_spec = lambda width: pl.BlockSpec((1, tm, width), lambda bi, i: (bi, i, 0))
    q_spec = pl.BlockSpec((1, ATTN_HEADS, tm, LANES), lambda bi, i: (bi, 0, i, 0))
    vt_shape = (b, s // KEY_CHUNK, ATTN_KV_HEADS, VT_ROWS, KEY_CHUNK)
    vt_spec = pl.BlockSpec((1, tm // KEY_CHUNK) + vt_shape[2:], lambda bi, i: (bi, i, 0, 0, 0))
    return pl.pallas_call(
        _inproj_body,
        out_shape=(
            jax.ShapeDtypeStruct((b, ATTN_HEADS, s, LANES), BF16),
            tok(KV_W, BF16),
            jax.ShapeDtypeStruct(vt_shape, BF16),
            tok(CONV_CH, F32),
            tok(MLSTM_W, BF16), tok(MLSTM_W, BF16), tok(MLSTM_W, BF16),
            tok(MLSTM_W, F32),
            tok(LANES, F32),
            jax.ShapeDtypeStruct((b, N_GATES, s), F32),
        ),
        grid=(b, s // tm),
        in_specs=[
            tok_spec(d),
            _const_spec((1, d)),
            _layer_spec((d, OFF_MG), layer),
            _layer_spec((d, LANES), layer),
            _layer_spec((N_GATES, d), layer),
            _const_spec((1, LANES)),
            _const_spec((N_GATES, 1)),
            _const_spec((1, ATTN_W)),
            _const_spec((1, LANES)),
            pl.BlockSpec((tm, LANES), lambda bi, i: (i, 0)),
            pl.BlockSpec((tm, LANES), lambda bi, i: (i, 0)),
            _const_spec(bd.shape),
        ],
        out_specs=(
            q_spec, tok_spec(KV_W), vt_spec, tok_spec(CONV_CH),
            tok_spec(MLSTM_W), tok_spec(MLSTM_W), tok_spec(MLSTM_W), tok_spec(MLSTM_W),
            tok_spec(LANES),
            pl.BlockSpec((1, N_GATES, tm), lambda bi, i: (bi, 0, i)),
        ),
        compiler_params=_cparams(("parallel", "parallel")),
        name="input_projection",
    )(x, g, w_in, w_gate, w_gate_t, gate_b, gate_bt, q_gain, k_gain, cos, sin, bd)


def _attn_body(q_ref, k_ref, vt_ref, o_ref, s_ref, mx_ref, qt_ref, *, unroll):
    tq = q_ref.shape[2]
    n_chunks = vt_ref.shape[1]
    tk = vt_ref.shape[4]

    for j in range(2):
        qt_ref[j] = q_ref[0, j].astype(F32).T.astype(BF16)

    def scores(c, slot):
        kc = k_ref[0, pl.ds(pl.multiple_of(c * tk, tk), tk), :]
        for j in range(2):
            s = jnp.dot(kc, qt_ref[j], preferred_element_type=F32)
            s_ref[slot, j] = s
            mx_ref[slot, j] = jnp.max(s, axis=0, keepdims=True)

    def consume(c, slot, state):
        vtc = vt_ref[0, c, 0]
        out = []
        for j in range(2):
            m, acc = state[j]
            s = s_ref[slot, j]
            m_new = jnp.maximum(m, mx_ref[slot, j])
            alpha = jnp.exp2(m - m_new)
            p = jnp.exp2(s - m_new).astype(BF16)
            out.append((m_new, alpha * acc + jnp.dot(vtc, p, preferred_element_type=F32)))
        return tuple(out)

    def step(i, state):
        for u in range(unroll):
            c = unroll * i + u
            scores(jnp.minimum(c + 1, n_chunks - 1), (u + 1) % 2)
            state = consume(c, u % 2, state)
        return state

    init = tuple((jnp.full((1, tq), -jnp.inf, F32), jnp.zeros((VT_ROWS, tq), F32)) for _ in range(2))
    scores(0, 0)
    (_, a0), (_, a1) = lax.fori_loop(0, n_chunks // unroll, step, init)
    o = [a[0:HEAD_DIM] / a[HEAD_DIM:HEAD_DIM + 1] for a in (a0, a1)]
    o_ref[0] = jnp.concatenate(o, axis=0).T.astype(BF16)


def _attention(q, k, vt, *, tq):
    b, _, s, _ = q.shape
    n_chunks, _, hd, tk = vt.shape[1:]
    unroll = max(u for u in (8, 4, 2) if n_chunks % u == 0)
    assert n_chunks % unroll == 0
    heads_per_kv = ATTN_HEADS // ATTN_KV_HEADS
    return pl.pallas_call(
        functools.partial(_attn_body, unroll=unroll),
        out_shape=jax.ShapeDtypeStruct((b, s, ATTN_W), BF16),
        grid=(b, ATTN_HEADS // 2, s // tq),
        in_specs=[pl.BlockSpec((1, 2, tq, LANES), lambda bi, p, i: (bi, p, i, 0)),
                  pl.BlockSpec((1, s, KV_W), lambda bi, p, i: (bi, 0, 0)),
                  pl.BlockSpec((1, n_chunks, 1, hd, tk), lambda bi, p, i: (bi, 0, (2 * p) // heads_per_kv, 0, 0))],
        out_specs=pl.BlockSpec((1, tq, LANES), lambda bi, p, i: (bi, i, p)),
        scratch_shapes=[pltpu.VMEM((2, 2, tk, tq), F32), pltpu.VMEM((2, 2, 1, tq), F32),
                        pltpu.VMEM((2, LANES, tq), BF16)],
        compiler_params=_cparams(("parallel", "parallel", "arbitrary")),
        name="gqa_attention",
    )(q, k, vt)


def _conv_body(prev_ref, cur_ref, next_ref, w_ref, b_ref, lg_ref, lb_ref, o_ref, shift_ref):
    i = pl.program_id(1)
    ts = cur_ref.shape[1]
    prev = jnp.where(i > 0, prev_ref[0], 0.0)
    nxt = jnp.where(i < pl.num_programs(1) - 1, next_ref[0], 0.0)
    win = jnp.concatenate([prev, cur_ref[0], nxt], axis=0)
    w = w_ref[...]
    acc = jnp.zeros((ts, CONV_CH), F32) + b_ref[...]
    first_off = HALO_ROWS - CONV_PAD
    span = shift_ref.shape[0] - ts
    assert span >= ((first_off + CONV_WIDTH - 1) // SUBLANES) * SUBLANES
    assert SUBLANES - 1 + ts + span <= win.shape[0]
    for r in range(SUBLANES):
        shift_ref[...] = win[r:r + ts + span, :]
        for t in range(CONV_WIDTH):
            off = first_off + t
            if off % SUBLANES == r:
                a = off - r
                acc = acc + shift_ref[a:a + ts, :] * w[t:t + 1, :]
    mu = jnp.mean(acc, axis=-1, keepdims=True)
    cen = acc - mu
    var = jnp.mean(cen * cen, axis=-1, keepdims=True)
    un = cen * lax.rsqrt(var + EPS) * lg_ref[...] + lb_ref[...]
    o_ref[0] = (un * _sigmoid(un)).astype(BF16)


def _conv(u, dw_w, dw_b, ln_g, ln_b, *, ts):
    b, s, c = u.shape
    r = ts // HALO_ROWS
    n_halo = s // HALO_ROWS
    return pl.pallas_call(
        _conv_body,
        out_shape=jax.ShapeDtypeStruct((b, s, c), BF16),
        grid=(b, s // ts),
        in_specs=[
            pl.BlockSpec((1, HALO_ROWS, c), lambda bi, i: (bi, jnp.maximum(i * r - 1, 0), 0)),
            pl.BlockSpec((1, ts, c), lambda bi, i: (bi, i, 0)),
            pl.BlockSpec((1, HALO_ROWS, c), lambda bi, i: (bi, jnp.minimum((i + 1) * r, n_halo - 1), 0)),
            _const_spec((CONV_WIDTH, c)),
            _const_spec((1, c)), _const_spec((1, c)), _const_spec((1, c)),
        ],
        out_specs=pl.BlockSpec((1, ts, c), lambda bi, i: (bi, i, 0)),
        scratch_shapes=[pltpu.VMEM((ts + 2 * HALO_ROWS - SUBLANES, c), F32)],
        compiler_params=_cparams(("parallel", "parallel")),
        name="conformer_conv",
    )(u, u, u, dw_w, dw_b, ln_g, ln_b)


def _expand_heads(cols):
    rows = cols[0].shape[0]
    hid = lax.broadcasted_iota(jnp.int32, (rows, MLSTM_W), 1) >> 6
    out = jnp.broadcast_to(cols[3], (rows, MLSTM_W))
    for h in (2, 1, 0):
        out = jnp.where(hid == h, cols[h], out)
    return out


def _split3(x):
    a = x.astype(BF16)
    r = x - a.astype(F32)
    b = r.astype(BF16)
    c = (r - b.astype(F32)).astype(BF16)
    return a, b, c


def _mlstm_direction(d, q, k, v, gate, gate_t, c_ref, n_ref, m_ref, bd1):
    si = d
    L = q.shape[0]
    r_i = lax.broadcasted_iota(jnp.int32, (L, L), 0)
    c_i = lax.broadcasted_iota(jnp.int32, (L, L), 1)
    keep = (c_i <= r_i) if d == 0 else (c_i >= r_i)
    tri = jnp.where(keep, 1.0, 0.0).astype(BF16)
    tri_t = jnp.where((r_i <= c_i) if d == 0 else (r_i >= c_i), 1.0, 0.0).astype(BF16)

    cum_col = sum(jnp.dot(tri, t, preferred_element_type=F32) for t in _split3(gate))
    cum_row = sum(jnp.dot(t, tri_t, preferred_element_type=F32) for t in _split3(gate_t))
    tot = jnp.sum(gate, axis=0, keepdims=True)

    hid = lax.broadcasted_iota(jnp.int32, (L, MLSTM_W), 1) >> 6
    m_prev_l = m_ref[si]
    qf = q.astype(F32)
    kf = k.astype(F32)

    m_t_c, inter_c, rs_c, wend_c, mnew_c, decay_c = [], [], [], [], [], []
    num_intra = jnp.zeros((L, MLSTM_W), F32)
    for h in range(MLSTM_HEADS):
        ji = d * 8 + h
        jf = d * 8 + MLSTM_HEADS + h
        m_prev = m_prev_l[:, h * MLSTM_HEAD_DIM:h * MLSTM_HEAD_DIM + 1]
        b_col = cum_col[:, jf:jf + 1]
        i_col = gate[:, ji:ji + 1]
        b_row = cum_row[jf:jf + 1, :]
        i_row = gate_t[ji:ji + 1, :]
        b_end = tot[:, jf:jf + 1]

        logw = jnp.where(keep, b_col - b_row + i_row, -jnp.inf)
        m_inter = b_col + m_prev
        m_t = jnp.maximum(jnp.max(logw, axis=-1, keepdims=True), m_inter)
        w = jnp.exp(logw - m_t)
        qm = jnp.where(hid == h, qf, 0.0).astype(BF16)
        s = lax.dot_general(qm, k, (((1,), (1,)), ((), ())), preferred_element_type=F32) * w
        r = jnp.dot(s.astype(BF16), v, preferred_element_type=F32)
        num_intra = jnp.where(hid == h, r, num_intra)
        m_t_c.append(m_t)
        inter_c.append(jnp.exp(m_inter - m_t))
        rs_c.append(jnp.sum(s, axis=-1, keepdims=True))

        logw_end = b_end - b_col + i_col
        m_new = jnp.maximum(b_end + m_prev, jnp.max(logw_end, axis=0, keepdims=True))
        wend_c.append(jnp.exp(logw_end - m_new))
        mnew_c.append(m_new)
        decay_c.append(jnp.exp(b_end + m_prev - m_new))

    m_t_l = _expand_heads(m_t_c)
    inter_l = _expand_heads(inter_c)
    c_mat = c_ref[si]
    n_vec = n_ref[si]
    c_hi = c_mat.astype(BF16)
    c_lo = (c_mat - c_hi.astype(F32)).astype(BF16)
    qc = (jnp.dot(q, c_hi, preferred_element_type=F32)
          + jnp.dot(q, c_lo, preferred_element_type=F32))
    qn_l = _group_mean(qf * n_vec, bd1)
    num = num_intra + inter_l * qc
    den = _expand_heads(rs_c) + inter_l * qn_l
    h_out = num / jnp.maximum(jnp.abs(den), jnp.exp(-m_t_l))

    decay_l = _expand_heads(decay_c)
    kw = kf * _expand_heads(wend_c)
    kv = lax.dot_general(kw.astype(BF16), v, (((0,), (0,)), ((), ())),
                         preferred_element_type=F32)
    rr = lax.broadcasted_iota(jnp.int32, (MLSTM_W, MLSTM_W), 0) >> 6
    cc = lax.broadcasted_iota(jnp.int32, (MLSTM_W, MLSTM_W), 1) >> 6
    c_ref[si] = decay_l * c_mat + jnp.where(rr == cc, kv, 0.0)
    n_ref[si] = decay_l * n_vec + jnp.sum(kw, axis=0, keepdims=True)
    m_ref[si] = _expand_heads(mnew_c)
    return h_out


def _mlstm_body(qf_ref, kf_ref, vf_ref, gf_ref, gtf_ref, qb_ref, kb_ref, vb_ref, gb_ref, gtb_ref,
                bd1_ref, hf_ref, hb_ref, c_ref, n_ref, m_ref):
    @pl.when(pl.program_id(1) == 0)
    def _():
        c_ref[...] = jnp.zeros_like(c_ref)
        n_ref[...] = jnp.zeros_like(n_ref)
        m_ref[...] = jnp.zeros_like(m_ref)

    bd1 = bd1_ref[...]
    hf_ref[0] = _mlstm_direction(0, qf_ref[0], kf_ref[0], vf_ref[0], gf_ref[0], gtf_ref[0],
                                 c_ref, n_ref, m_ref, bd1)
    hb_ref[0] = _mlstm_direction(1, qb_ref[0], kb_ref[0], vb_ref[0], gb_ref[0], gtb_ref[0],
                                 c_ref, n_ref, m_ref, bd1)


def _mlstm(mq, mk, mv, gate, gate_t, bd1):
    b, s, w = mq.shape
    L = MLSTM_CHUNK
    nc = s // L
    fwd = pl.BlockSpec((1, L, w), lambda bi, c: (bi, c, 0))
    bwd = pl.BlockSpec((1, L, w), lambda bi, c: (bi, nc - 1 - c, 0))
    gfwd = pl.BlockSpec((1, L, LANES), lambda bi, c: (bi, c, 0))
    gbwd = pl.BlockSpec((1, L, LANES), lambda bi, c: (bi, nc - 1 - c, 0))
    gtfwd = pl.BlockSpec((1, N_GATES, L), lambda bi, c: (bi, 0, c))
    gtbwd = pl.BlockSpec((1, N_GATES, L), lambda bi, c: (bi, 0, nc - 1 - c))
    n_scan = 2
    return pl.pallas_call(
        _mlstm_body,
        out_shape=(jax.ShapeDtypeStruct((b, s, w), F32), jax.ShapeDtypeStruct((b, s, w), F32)),
        grid=(b, nc),
        in_specs=[fwd, fwd, fwd, gfwd, gtfwd, bwd, bwd, bwd, gbwd, gtbwd, _const_spec(bd1.shape)],
        out_specs=(fwd, bwd),
        scratch_shapes=[pltpu.VMEM((n_scan, w, w), F32), pltpu.VMEM((n_scan, 1, w), F32),
                        pltpu.VMEM((n_scan, 1, w), F32)],
        compiler_params=_cparams(("parallel", "arbitrary")),
        name="mlstm_scan",
    )(mq, mk, mv, gate, gate_t, mq, mk, mv, gate, gate_t, bd1)


def _memkv_body(mem_ref, g_ref, w_ref, kg_ref, k_ref, v_ref):
    mb = _rms(mem_ref[0], g_ref[...]).astype(BF16)
    kv = jnp.dot(mb, w_ref[...], preferred_element_type=F32)
    d = kv.shape[1] // 2
    hd = d // XATTN_HEADS
    for h in range(XATTN_HEADS):
        k_ref[0, h * hd:(h + 1) * hd, :] = _rms(kv[:, h * hd:(h + 1) * hd], kg_ref[...]).T.astype(BF16)
    v_ref[0] = kv[:, d:].astype(BF16)


def _memkv(mem, g, wkv, layer, k_gain):
    b, m, d = mem.shape
    spec = pl.BlockSpec((1, m, d), lambda bi: (bi, 0, 0))
    return pl.pallas_call(
        _memkv_body,
        out_shape=(jax.ShapeDtypeStruct((b, d, m), BF16), jax.ShapeDtypeStruct((b, m, d), BF16)),
        grid=(b,),
        in_specs=[spec, _const_spec((1, d)), _layer_spec(wkv.shape[1:], layer), _const_spec(k_gain.shape)],
        out_specs=(pl.BlockSpec((1, d, m), lambda bi: (bi, 0, 0)), spec),
        compiler_params=_cparams(("parallel",)),
        name="memory_kv",
    )(mem, g, wkv, k_gain)


def _mix_out_body(x_ref, ya_ref, yc_ref, hf_ref, hb_ref, og_ref, mg_ref, bd_ref, wo_ref,
                  xg_ref, wq_ref, qg_ref, kt_ref, v_ref, wxo_ref, o_ref):
    hs = hf_ref[0] + hb_ref[0]
    hn = hs * lax.rsqrt(_group_mean(hs * hs, bd_ref[...]) + EPS) * mg_ref[...]
    ym = (og_ref[0] * hn).astype(BF16)
    x = x_ref[0]
    x = (x + jnp.dot(ya_ref[0], wo_ref[0:ATTN_W, :], preferred_element_type=F32)
         + jnp.dot(yc_ref[0], wo_ref[ATTN_W:ATTN_W + CONV_CH, :], preferred_element_type=F32)
         + jnp.dot(ym, wo_ref[ATTN_W + CONV_CH:, :], preferred_element_type=F32))

    xb = _rms(x, xg_ref[...]).astype(BF16)
    qx = jnp.dot(xb, wq_ref[...], preferred_element_type=F32)
    hd = qx.shape[1] // XATTN_HEADS
    outs = []
    for h in range(XATTN_HEADS):
        sl = slice(h * hd, (h + 1) * hd)
        qh = (_rms(qx[:, sl], qg_ref[...]) * (hd ** -0.5)).astype(BF16)
        s = jnp.dot(qh, kt_ref[0, sl, :], preferred_element_type=F32)
        p = jnp.exp(s - jnp.max(s, axis=-1, keepdims=True))
        l = jnp.sum(p, axis=-1, keepdims=True)
        outs.append((jnp.dot(p.astype(BF16), v_ref[0, :, sl], preferred_element_type=F32) / l).astype(BF16))
    o = jnp.concatenate(outs, axis=1)
    o_ref[0] = x + jnp.dot(o, wxo_ref[...], preferred_element_type=F32)


def _mix_out(x, ya, yc, hf, hb, og, m_gain, bd, w_out, x_gain, wq, q_gain, mk, mv, wo, layer, *, tm):
    b, s, d = x.shape
    tok = lambda width: pl.BlockSpec((1, tm, width), lambda bi, i: (bi, i, 0))
    mem_spec = lambda a: pl.BlockSpec((1,) + a.shape[1:], lambda bi, i: (bi, 0, 0))
    return pl.pallas_call(
        _mix_out_body,
        out_shape=jax.ShapeDtypeStruct((b, s, d), F32),
        grid=(b, s // tm),
        in_specs=[tok(d), tok(ATTN_W), tok(CONV_CH), tok(MLSTM_W), tok(MLSTM_W), tok(MLSTM_W),
                  _const_spec(m_gain.shape), _const_spec(bd.shape), _layer_spec(w_out.shape[1:], layer),
                  _const_spec(x_gain.shape), _layer_spec(wq.shape[1:], layer), _const_spec(q_gain.shape),
                  mem_spec(mk), mem_spec(mv), _layer_spec(wo.shape[1:], layer)],
        out_specs=tok(d),
        compiler_params=_cparams(("parallel", "parallel")),
        name="mix_out_cross_attention",
    )(x, ya, yc, hf, hb, og, m_gain, bd, w_out, x_gain, wq, q_gain, mk, mv, wo)


def _rope_tables(s_len):
    rows = s_len // GRID_W
    row_idx = jnp.repeat(jnp.arange(rows, dtype=jnp.int32), GRID_W).astype(F32)
    col_idx = jnp.tile(jnp.arange(GRID_W, dtype=jnp.int32), rows).astype(F32)
    n_freq = HEAD_DIM // 4
    inv_freq = jnp.float32(ROPE_THETA) ** (-jnp.arange(n_freq, dtype=F32) / n_freq)
    ang = jnp.concatenate([row_idx[:, None] * inv_freq, col_idx[:, None] * inv_freq], axis=-1)
    cos = jnp.repeat(jnp.cos(ang), 2, axis=-1)
    sin = jnp.repeat(jnp.sin(ang), 2, axis=-1) * jnp.tile(jnp.array([-1.0, 1.0], F32), HEAD_DIM // 2)
    return jnp.tile(cos, (1, 2)), jnp.tile(sin, (1, 2))


def _block_diag(scale):
    idx = jnp.arange(MLSTM_W) // MLSTM_HEAD_DIM
    return jnp.where(idx[:, None] == idx[None, :], scale, 0.0).astype(BF16)


def _pick_tile(n, pref):
    t = min(n, pref)
    while n % t:
        t //= 2
    return t


def kernel(x, mem, ffn1_norm, ffn1_w13, ffn1_w2, mix_norm, w_in, attn_q_norm, attn_k_norm, conv_dw_w,
           conv_dw_b, conv_ln_g, conv_ln_b, mlstm_gate_b, mlstm_out_norm, w_out, xattn_norm, mem_norm,
           xattn_wq, xattn_wkv, xattn_q_norm, xattn_k_norm, xattn_wo, ffn2_norm, ffn2_w13, ffn2_w2):
    b, s, d = x.shape
    depth = w_in.shape[0]
    cos, sin = _rope_tables(s)
    bd_mean = _block_diag(1.0 / MLSTM_HEAD_DIM)
    bd_sum = _block_diag(1.0)

    tm_ffn = _pick_tile(b * s, 512)
    tm_tok = _pick_tile(s, 512)
    tq = _pick_tile(s, 512)
    ts = _pick_tile(s, 256)
    row = lambda v: v.reshape(1, -1)

    bf = lambda w: w.astype(BF16)
    ffn1_w13, ffn1_w2, ffn2_w13, ffn2_w2 = bf(ffn1_w13), bf(ffn1_w2), bf(ffn2_w13), bf(ffn2_w2)
    w_out, xattn_wq, xattn_wkv, xattn_wo = bf(w_out), bf(xattn_wq), bf(xattn_wkv), bf(xattn_wo)
    w_gate = w_in[:, :, OFF_MG:]
    w_gate_pad = bf(jnp.pad(w_gate, ((0, 0), (0, 0), (0, LANES - N_GATES))))
    w_gate_t = bf(jnp.swapaxes(w_gate, 1, 2))
    w_in = bf(w_in)

    for l in range(depth):
        x = _ffn(x.reshape(b * s, d), row(ffn1_norm[l]), ffn1_w13, ffn1_w2, l,
                 tm=tm_ffn, n_chunks=FFN_CHUNKS).reshape(b, s, d)

        gate_b = mlstm_gate_b[l].reshape(1, N_GATES)
        q, kk, vt, u, mq, mk, mv, og, gate, gate_t = _inproj(
            x, row(mix_norm[l]), w_in, w_gate_pad, w_gate_t, l,
            jnp.pad(gate_b, ((0, 0), (0, LANES - N_GATES))), gate_b.reshape(N_GATES, 1),
            row(jnp.tile(attn_q_norm[l], ATTN_HEADS)), row(jnp.tile(attn_k_norm[l], ATTN_KV_HEADS)),
            cos, sin, bd_mean, tm=tm_tok)

        ya = _attention(q, kk, vt, tq=tq)
        yc = _conv(u, conv_dw_w[l], row(conv_dw_b[l]), row(conv_ln_g[l]), row(conv_ln_b[l]), ts=ts)
        hf, hb = _mlstm(mq, mk, mv, gate, gate_t, bd_sum)
        mem_k, mem_v = _memkv(mem, row(mem_norm[l]), xattn_wkv, l, row(xattn_k_norm[l]))
        x = _mix_out(x, ya, yc, hf, hb, og, row(mlstm_out_norm[l]), bd_mean, w_out,
                     row(xattn_norm[l]), xattn_wq, row(xattn_q_norm[l]), mem_k, mem_v, xattn_wo, l, tm=tm_tok)

        x = _ffn(x.reshape(b * s, d), row(ffn2_norm[l]), ffn2_w13, ffn2_w2, l,
                 tm=tm_ffn, n_chunks=FFN_CHUNKS).reshape(b, s, d)
    return x
```

```python
import functools

import jax
import jax.numpy as jnp
from jax import lax
from jax.experimental import pallas as pl
from jax.experimental.pallas import tpu as pltpu

F32 = jnp.float32
BF16 = jnp.bfloat16

GRID_W = 64
EPS = 1e-6
LOG2E = 1.4426950408889634
ATTN_HEADS = 8
ATTN_KV_HEADS = 2
HEAD_DIM = 64
ATTN_W = ATTN_HEADS * HEAD_DIM
KV_W = ATTN_KV_HEADS * HEAD_DIM
ROPE_THETA = 10000.0
CONV_CH = 256
CONV_WIDTH = 31
CONV_PAD = CONV_WIDTH // 2
MLSTM_HEADS = 4
MLSTM_HEAD_DIM = 64
MLSTM_W = MLSTM_HEADS * MLSTM_HEAD_DIM
MLSTM_CHUNK = 128
N_GATES = 16
XATTN_HEADS = 4
OFF_K = ATTN_W
OFF_V = OFF_K + KV_W
OFF_CONV = OFF_V + KV_W
OFF_MQ = OFF_CONV + 2 * CONV_CH
OFF_MK = OFF_MQ + MLSTM_W
OFF_MV = OFF_MK + MLSTM_W
OFF_MO = OFF_MV + MLSTM_W
OFF_MG = OFF_MO + MLSTM_W

LANES = 128
SUBLANES = 8
HALO_ROWS = 16
KEY_CHUNK = 512
FFN_CHUNKS = 1
BF16_ROWS = 16
VT_ROWS = HEAD_DIM + BF16_ROWS
VMEM_LIMIT = 56 * 1024 * 1024


def _cparams(sem):
    return pltpu.CompilerParams(dimension_semantics=sem, vmem_limit_bytes=VMEM_LIMIT)


def _const_spec(shape):
    nd = len(shape)
    return pl.BlockSpec(shape, lambda *_: (0,) * nd)


def _layer_spec(block2d, layer, **kw):
    return pl.BlockSpec((None,) + tuple(block2d), lambda *_: (layer, 0, 0), **kw)


def _rms(x, g):
    return x * lax.rsqrt(jnp.mean(x * x, axis=-1, keepdims=True) + EPS) * g


def _group_mean(sq, bd, split=True):
    hi = sq.astype(BF16)
    out = jnp.dot(hi, bd, preferred_element_type=F32)
    if split:
        lo = (sq - hi.astype(F32)).astype(BF16)
        out = out + jnp.dot(lo, bd, preferred_element_type=F32)
    return out


def _swap_pairs(x):
    nxt = pltpu.roll(x, LANES - 1, axis=1)
    prv = pltpu.roll(x, 1, axis=1)
    lane = lax.broadcasted_iota(jnp.int32, x.shape, 1)
    return jnp.where((lane & 1) == 0, nxt, prv)


def _sigmoid(x):
    return 1.0 / (1.0 + jnp.exp(-x))


def _ffn_body(x_ref, g_ref, w13_ref, w2_ref, o_ref, *, d_ff, n_chunks):
    x = x_ref[...]
    xb = _rms(x, g_ref[...]).astype(BF16)
    fc = d_ff // n_chunks
    acc = None
    for c in range(n_chunks):
        a = jnp.dot(xb, w13_ref[:, c * fc:(c + 1) * fc], preferred_element_type=F32)
        b = jnp.dot(xb, w13_ref[:, d_ff + c * fc:d_ff + (c + 1) * fc], preferred_element_type=F32)
        act = (a * _sigmoid(a) * b).astype(BF16)
        part = jnp.dot(act, w2_ref[c * fc:(c + 1) * fc, :], preferred_element_type=F32)
        acc = part if acc is None else acc + part
    o_ref[...] = x + 0.5 * acc


def _ffn(x2d, g, w13, w2, layer, *, tm, n_chunks):
    n, d = x2d.shape
    d_ff = w2.shape[1]
    return pl.pallas_call(
        functools.partial(_ffn_body, d_ff=d_ff, n_chunks=n_chunks),
        out_shape=jax.ShapeDtypeStruct((n, d), F32),
        grid=(n // tm,),
        in_specs=[
            pl.BlockSpec((tm, d), lambda i: (i, 0)),
            _const_spec((1, d)),
            _layer_spec((d, 2 * d_ff), layer, pipeline_mode=pl.Buffered(1)),
            _layer_spec((d_ff, d), layer, pipeline_mode=pl.Buffered(1)),
        ],
        out_specs=pl.BlockSpec((tm, d), lambda i: (i, 0)),
        compiler_params=_cparams(("parallel",)),
        name="swiglu_half_step",
    )(x2d, g, w13, w2)


def _inproj_body(x_ref, g_ref, w_ref, wg_ref, wgt_ref, gb_ref, gbt_ref, qg_ref, kg_ref,
                 cos_ref, sin_ref, bd_ref,
                 q_ref, k_ref, vt_ref, u_ref, mq_ref, mk_ref, mv_ref, og_ref, gate_ref, gatet_ref):
    xb = _rms(x_ref[0], g_ref[...]).astype(BF16)
    h = jnp.dot(xb, w_ref[...], preferred_element_type=F32)
    bd = bd_ref[...]
    cos = cos_ref[...]
    sin = sin_ref[...]

    hq = h[:, 0:ATTN_W]
    ms = jnp.concatenate([_group_mean(hq[:, 0:256] * hq[:, 0:256], bd, split=False),
                          _group_mean(hq[:, 256:512] * hq[:, 256:512], bd, split=False)], axis=1)
    qn = hq * lax.rsqrt(ms + EPS) * qg_ref[...]
    in_first = lax.broadcasted_iota(jnp.int32, cos.shape, 1) < HEAD_DIM
    group = ATTN_HEADS // ATTN_KV_HEADS
    for j in range(ATTN_W // LANES):
        z = qn[:, j * LANES:(j + 1) * LANES]
        r = (z * cos + _swap_pairs(z) * sin) * (HEAD_DIM ** -0.5 * LOG2E)
        rolled = pltpu.roll(r, HEAD_DIM, axis=1)
        for half in range(2):
            hd = 2 * j + half
            g = hd // group
            z = r if half == g else rolled
            keep = jnp.where(in_first, z, 0.0) if g == 0 else jnp.where(in_first, 0.0, z)
            q_ref[0, hd] = keep.astype(BF16)

    hk = h[:, OFF_K:OFF_V]
    msk = _group_mean(hk * hk, bd_ref[0:LANES, 0:LANES], split=False)
    kn = hk * lax.rsqrt(msk + EPS) * kg_ref[...]
    k_ref[0] = (kn * cos + _swap_pairs(kn) * sin).astype(BF16)
    hv = h[:, OFF_V:OFF_CONV]
    for r in range(hv.shape[0] // KEY_CHUNK):
        vt = hv[r * KEY_CHUNK:(r + 1) * KEY_CHUNK, :].T.astype(BF16)
        for g in range(ATTN_KV_HEADS):
            vt_ref[0, r, g, 0:HEAD_DIM] = vt[g * HEAD_DIM:(g + 1) * HEAD_DIM, :]
            vt_ref[0, r, g, HEAD_DIM:VT_ROWS] = jnp.ones((VT_ROWS - HEAD_DIM, KEY_CHUNK), BF16)

    u_ref[0] = h[:, OFF_CONV:OFF_CONV + CONV_CH] * _sigmoid(h[:, OFF_CONV + CONV_CH:OFF_MQ])

    mq_ref[0] = h[:, OFF_MQ:OFF_MK].astype(BF16)
    mk_ref[0] = (h[:, OFF_MK:OFF_MV] * (MLSTM_HEAD_DIM ** -0.5)).astype(BF16)
    mv_ref[0] = h[:, OFF_MV:OFF_MO].astype(BF16)
    og_ref[0] = _sigmoid(h[:, OFF_MO:OFF_MG])

    def log_gate(pre, is_forget):
        ls = jnp.minimum(pre, 0.0) - jnp.log1p(jnp.exp(-jnp.abs(pre)))
        return jnp.where(is_forget, ls, pre)

    gp = jnp.dot(xb, wg_ref[...], preferred_element_type=F32) + gb_ref[...]
    col = lax.broadcasted_iota(jnp.int32, gp.shape, 1)
    gate_ref[0] = log_gate(gp, ((col >> 2) & 1) == 1)
    gpt = lax.dot_general(wgt_ref[...], xb, (((1,), (1,)), ((), ())),
                          preferred_element_type=F32) + gbt_ref[...]
    row = lax.broadcasted_iota(jnp.int32, gpt.shape, 0)
    gatet_ref[0] = log_gate(gpt, ((row >> 2) & 1) == 1)


def _inproj(x, g, w_in, w_gate, w_gate_t, layer, gate_b, gate_bt, q_gain, k_gain, cos, sin, bd, *, tm):
    b, s, d = x.shape
    tok = lambda width, dt: jax.ShapeDtypeStruct((b, s, width), dt)
    tok_spec = lambda width: pl.BlockSpec((1, tm, width), lambda bi, i: (bi, i, 0))
    q_spec = pl.BlockSpec((1, ATTN_HEADS, tm, LANES), lambda bi, i: (bi, 0, i, 0))
    vt_shape = (b, s // KEY_CHUNK, ATTN_KV_HEADS, VT_ROWS, KEY_CHUNK)
    vt_spec = pl.BlockSpec((1, tm // KEY_CHUNK) + vt_shape[2:], lambda bi, i: (bi, i, 0, 0, 0))
    return pl.pallas_call(
        _inproj_body,
        out_shape=(
            jax.ShapeDtypeStruct((b, ATTN_HEADS, s, LANES), BF16),
            tok(KV_W, BF16),
            jax.ShapeDtypeStruct(vt_shape, BF16),
            tok(CONV_CH, F32),
            tok(MLSTM_W, BF16), tok(MLSTM_W, BF16), tok(MLSTM_W, BF16),
            tok(MLSTM_W, F32),
            tok(LANES, F32),
            jax.ShapeDtypeStruct((b, N_GATES, s), F32),
        ),
        grid=(b, s // tm),
        in_specs=[
            tok_spec(d),
            _const_spec((1, d)),
            _layer_spec((d, OFF_MG), layer),
            _layer_spec((d, LANES), layer),
            _layer_spec((N_GATES, d), layer),
            _const_spec((1, LANES)),
            _const_spec((N_GATES, 1)),
            _const_spec((1, ATTN_W)),
            _const_spec((1, LANES)),
            pl.BlockSpec((tm, LANES), lambda bi, i: (i, 0)),
            pl.BlockSpec((tm, LANES), lambda bi, i: (i, 0)),
            _const_spec(bd.shape),
        ],
        out_specs=(
            q_spec, tok_spec(KV_W), vt_spec, tok_spec(CONV_CH),
            tok_spec(MLSTM_W), tok_spec(MLSTM_W), tok_spec(MLSTM_W), tok_spec(MLSTM_W),
            tok_spec(LANES),
            pl.BlockSpec((1, N_GATES, tm), lambda bi, i: (bi, 0, i)),
        ),
        compiler_params=_cparams(("parallel", "parallel")),
        name="input_projection",
    )(x, g, w_in, w_gate, w_gate_t, gate_b, gate_bt, q_gain, k_gain, cos, sin, bd)


def _attn_body(q_ref, k_ref, vt_ref, o_ref, s_ref, mx_ref, qt_ref, *, unroll):
    tq = q_ref.shape[2]
    n_chunks = vt_ref.shape[1]
    tk = vt_ref.shape[4]

    for j in range(2):
        qt_ref[j] = q_ref[0, j].astype(F32).T.astype(BF16)

    def scores(c, slot):
        kc = k_ref[0, pl.ds(pl.multiple_of(c * tk, tk), tk), :]
        for j in range(2):
            s = jnp.dot(kc, qt_ref[j], preferred_element_type=F32)
            s_ref[slot, j] = s
            mx_ref[slot, j] = jnp.max(s, axis=0, keepdims=True)

    def consume(c, slot, state):
        vtc = vt_ref[0, c, 0]
        out = []
        for j in range(2):
            m, acc = state[j]
            s = s_ref[slot, j]
            m_new = jnp.maximum(m, mx_ref[slot, j])
            alpha = jnp.exp2(m - m_new)
            p = jnp.exp2(s - m_new).astype(BF16)
            out.append((m_new, alpha * acc + jnp.dot(vtc, p, preferred_element_type=F32)))
        return tuple(out)

    def step(i, state):
        for u in range(unroll):
            c = unroll * i + u
            scores(jnp.minimum(c + 1, n_chunks - 1), (u + 1) % 2)
            state = consume(c, u % 2, state)
        return state

    init = tuple((jnp.full((1, tq), -jnp.inf, F32), jnp.zeros((VT_ROWS, tq), F32)) for _ in range(2))
    scores(0, 0)
    (_, a0), (_, a1) = lax.fori_loop(0, n_chunks // unroll, step, init)
    o = [a[0:HEAD_DIM] / a[HEAD_DIM:HEAD_DIM + 1] for a in (a0, a1)]
    o_ref[0] = jnp.concatenate(o, axis=0).T.astype(BF16)


def _attention(q, k, vt, *, tq):
    b, _, s, _ = q.shape
    n_chunks, _, hd, tk = vt.shape[1:]
    unroll = max([u for u in (8, 4, 2) if n_chunks % u == 0 and n_chunks // u >= 2] or [2])
    assert n_chunks % unroll == 0
    heads_per_kv = ATTN_HEADS // ATTN_KV_HEADS
    return pl.pallas_call(
        functools.partial(_attn_body, unroll=unroll),
        out_shape=jax.ShapeDtypeStruct((b, s, ATTN_W), BF16),
        grid=(b, ATTN_HEADS // 2, s // tq),
        in_specs=[pl.BlockSpec((1, 2, tq, LANES), lambda bi, p, i: (bi, p, i, 0)),
                  pl.BlockSpec((1, s, KV_W), lambda bi, p, i: (bi, 0, 0)),
                  pl.BlockSpec((1, n_chunks, 1, hd, tk), lambda bi, p, i: (bi, 0, (2 * p) // heads_per_kv, 0, 0))],
        out_specs=pl.BlockSpec((1, tq, LANES), lambda bi, p, i: (bi, i, p)),
        scratch_shapes=[pltpu.VMEM((2, 2, tk, tq), F32), pltpu.VMEM((2, 2, 1, tq), F32),
                        pltpu.VMEM((2, LANES, tq), BF16)],
        compiler_params=_cparams(("parallel", "parallel", "arbitrary")),
        name="gqa_attention",
    )(q, k, vt)


def _conv_body(prev_ref, cur_ref, next_ref, w_ref, b_ref, lg_ref, lb_ref, o_ref, shift_ref):
    i = pl.program_id(1)
    ts = cur_ref.shape[1]
    prev = jnp.where(i > 0, prev_ref[0], 0.0)
    nxt = jnp.where(i < pl.num_programs(1) - 1, next_ref[0], 0.0)
    win = jnp.concatenate([prev, cur_ref[0], nxt], axis=0)
    w = w_ref[...]
    acc = jnp.zeros((ts, CONV_CH), F32) + b_ref[...]
    first_off = HALO_ROWS - CONV_PAD
    span = shift_ref.shape[0] - ts
    assert span >= ((first_off + CONV_WIDTH - 1) // SUBLANES) * SUBLANES
    assert SUBLANES - 1 + ts + span <= win.shape[0]
    for r in range(SUBLANES):
        shift_ref[...] = win[r:r + ts + span, :]
        for t in range(CONV_WIDTH):
            off = first_off + t
            if off % SUBLANES == r:
                a = off - r
                acc = acc + shift_ref[a:a + ts, :] * w[t:t + 1, :]
    mu = jnp.mean(acc, axis=-1, keepdims=True)
    cen = acc - mu
    var = jnp.mean(cen * cen, axis=-1, keepdims=True)
    un = cen * lax.rsqrt(var + EPS) * lg_ref[...] + lb_ref[...]
    o_ref[0] = (un * _sigmoid(un)).astype(BF16)


def _conv(u, dw_w, dw_b, ln_g, ln_b, *, ts):
    b, s, c = u.shape
    r = ts // HALO_ROWS
    n_halo = s // HALO_ROWS
    return pl.pallas_call(
        _conv_body,
        out_shape=jax.ShapeDtypeStruct((b, s, c), BF16),
        grid=(b, s // ts),
        in_specs=[
            pl.BlockSpec((1, HALO_ROWS, c), lambda bi, i: (bi, jnp.maximum(i * r - 1, 0), 0)),
            pl.BlockSpec((1, ts, c), lambda bi, i: (bi, i, 0)),
            pl.BlockSpec((1, HALO_ROWS, c), lambda bi, i: (bi, jnp.minimum((i + 1) * r, n_halo - 1), 0)),
            _const_spec((CONV_WIDTH, c)),
            _const_spec((1, c)), _const_spec((1, c)), _const_spec((1, c)),
        ],
        out_specs=pl.BlockSpec((1, ts, c), lambda bi, i: (bi, i, 0)),
        scratch_shapes=[pltpu.VMEM((ts + 2 * HALO_ROWS - SUBLANES, c), F32)],
        compiler_params=_cparams(("parallel", "parallel")),
        name="conformer_conv",
    )(u, u, u, dw_w, dw_b, ln_g, ln_b)


def _expand_heads(cols):
    rows = cols[0].shape[0]
    hid = lax.broadcasted_iota(jnp.int32, (rows, MLSTM_W), 1) >> 6
    out = jnp.broadcast_to(cols[3], (rows, MLSTM_W))
    for h in (2, 1, 0):
        out = jnp.where(hid == h, cols[h], out)
    return out


def _split3(x):
    a = x.astype(BF16)
    r = x - a.astype(F32)
    b = r.astype(BF16)
    c = (r - b.astype(F32)).astype(BF16)
    return a, b, c


def _mlstm_direction(d, q, k, v, gate, gate_t, c_ref, n_ref, m_ref, bd1):
    si = d
    L = q.shape[0]
    r_i = lax.broadcasted_iota(jnp.int32, (L, L), 0)
    c_i = lax.broadcasted_iota(jnp.int32, (L, L), 1)
    keep = (c_i <= r_i) if d == 0 else (c_i >= r_i)
    tri = jnp.where(keep, 1.0, 0.0).astype(BF16)
    tri_t = jnp.where((r_i <= c_i) if d == 0 else (r_i >= c_i), 1.0, 0.0).astype(BF16)

    cum_col = sum(jnp.dot(tri, t, preferred_element_type=F32) for t in _split3(gate))
    cum_row = sum(jnp.dot(t, tri_t, preferred_element_type=F32) for t in _split3(gate_t))
    tot = jnp.sum(gate, axis=0, keepdims=True)

    hid = lax.broadcasted_iota(jnp.int32, (L, MLSTM_W), 1) >> 6
    m_prev_l = m_ref[si]
    qf = q.astype(F32)
    kf = k.astype(F32)

    m_t_c, inter_c, rs_c, wend_c, mnew_c, decay_c = [], [], [], [], [], []
    num_intra = jnp.zeros((L, MLSTM_W), F32)
    for h in range(MLSTM_HEADS):
        ji = d * 8 + h
        jf = d * 8 + MLSTM_HEADS + h
        m_prev = m_prev_l[:, h * MLSTM_HEAD_DIM:h * MLSTM_HEAD_DIM + 1]
        b_col = cum_col[:, jf:jf + 1]
        i_col = gate[:, ji:ji + 1]
        b_row = cum_row[jf:jf + 1, :]
        i_row = gate_t[ji:ji + 1, :]
        b_end = tot[:, jf:jf + 1]

        logw = jnp.where(keep, b_col - b_row + i_row, -jnp.inf)
        m_inter = b_col + m_prev
        m_t = jnp.maximum(jnp.max(logw, axis=-1, keepdims=True), m_inter)
        w = jnp.exp(logw - m_t)
        qm = jnp.where(hid == h, qf, 0.0).astype(BF16)
        s = lax.dot_general(qm, k, (((1,), (1,)), ((), ())), preferred_element_type=F32) * w
        r = jnp.dot(s.astype(BF16), v, preferred_element_type=F32)
        num_intra = jnp.where(hid == h, r, num_intra)
        m_t_c.append(m_t)
        inter_c.append(jnp.exp(m_inter - m_t))
        rs_c.append(jnp.sum(s, axis=-1, keepdims=True))

        logw_end = b_end - b_col + i_col
        m_new = jnp.maximum(b_end + m_prev, jnp.max(logw_end, axis=0, keepdims=True))
        wend_c.append(jnp.exp(logw_end - m_new))
        mnew_c.append(m_new)
        decay_c.append(jnp.exp(b_end + m_prev - m_new))

    m_t_l = _expand_heads(m_t_c)
    inter_l = _expand_heads(inter_c)
    c_mat = c_ref[si]
    n_vec = n_ref[si]
    c_hi = c_mat.astype(BF16)
    c_lo = (c_mat - c_hi.astype(F32)).astype(BF16)
    qc = (jnp.dot(q, c_hi, preferred_element_type=F32)
          + jnp.dot(q, c_lo, preferred_element_type=F32))
    qn_l = _group_mean(qf * n_vec, bd1)
    num = num_intra + inter_l * qc
    den = _expand_heads(rs_c) + inter_l * qn_l
    h_out = num / jnp.maximum(jnp.abs(den), jnp.exp(-m_t_l))

    decay_l = _expand_heads(decay_c)
    kw = kf * _expand_heads(wend_c)
    kv = lax.dot_general(kw.astype(BF16), v, (((0,), (0,)), ((), ())),
                         preferred_element_type=F32)
    rr = lax.broadcasted_iota(jnp.int32, (MLSTM_W, MLSTM_W), 0) >> 6
    cc = lax.broadcasted_iota(jnp.int32, (MLSTM_W, MLSTM_W), 1) >> 6
    c_ref[si] = decay_l * c_mat + jnp.where(rr == cc, kv, 0.0)
    n_ref[si] = decay_l * n_vec + jnp.sum(kw, axis=0, keepdims=True)
    m_ref[si] = _expand_heads(mnew_c)
    return h_out


def _mlstm_body(qf_ref, kf_ref, vf_ref, gf_ref, gtf_ref, qb_ref, kb_ref, vb_ref, gb_ref, gtb_ref,
                bd1_ref, hf_ref, hb_ref, c_ref, n_ref, m_ref):
    @pl.when(pl.program_id(1) == 0)
    def _():
        c_ref[...] = jnp.zeros_like(c_ref)
        n_ref[...] = jnp.zeros_like(n_ref)
        m_ref[...] = jnp.zeros_like(m_ref)

    bd1 = bd1_ref[...]
    hf_ref[0] = _mlstm_direction(0, qf_ref[0], kf_ref[0], vf_ref[0], gf_ref[0], gtf_ref[0],
                                 c_ref, n_ref, m_ref, bd1)
    hb_ref[0] = _mlstm_direction(1, qb_ref[0], kb_ref[0], vb_ref[0], gb_ref[0], gtb_ref[0],
                                 c_ref, n_ref, m_ref, bd1)


def _mlstm(mq, mk, mv, gate, gate_t, bd1):
    b, s, w = mq.shape
    L = MLSTM_CHUNK
    nc = s // L
    fwd = pl.BlockSpec((1, L, w), lambda bi, c: (bi, c, 0))
    bwd = pl.BlockSpec((1, L, w), lambda bi, c: (bi, nc - 1 - c, 0))
    gfwd = pl.BlockSpec((1, L, LANES), lambda bi, c: (bi, c, 0))
    gbwd = pl.BlockSpec((1, L, LANES), lambda bi, c: (bi, nc - 1 - c, 0))
    gtfwd = pl.BlockSpec((1, N_GATES, L), lambda bi, c: (bi, 0, c))
    gtbwd = pl.BlockSpec((1, N_GATES, L), lambda bi, c: (bi, 0, nc - 1 - c))
    n_scan = 2
    return pl.pallas_call(
        _mlstm_body,
        out_shape=(jax.ShapeDtypeStruct((b, s, w), F32), jax.ShapeDtypeStruct((b, s, w), F32)),
        grid=(b, nc),
        in_specs=[fwd, fwd, fwd, gfwd, gtfwd, bwd, bwd, bwd, gbwd, gtbwd, _const_spec(bd1.shape)],
        out_specs=(fwd, bwd),
        scratch_shapes=[pltpu.VMEM((n_scan, w, w), F32), pltpu.VMEM((n_scan, 1, w), F32),
                        pltpu.VMEM((n_scan, 1, w), F32)],
        compiler_params=_cparams(("parallel", "arbitrary")),
        name="mlstm_scan",
    )(mq, mk, mv, gate, gate_t, mq, mk, mv, gate, gate_t, bd1)


def _memkv_body(mem_ref, g_ref, w_ref, kg_ref, k_ref, v_ref):
    mb = _rms(mem_ref[0], g_ref[...]).astype(BF16)
    kv = jnp.dot(mb, w_ref[...], preferred_element_type=F32)
    d = kv.shape[1] // 2
    hd = d // XATTN_HEADS
    for h in range(XATTN_HEADS):
        k_ref[0, h * hd:(h + 1) * hd, :] = _rms(kv[:, h * hd:(h + 1) * hd], kg_ref[...]).T.astype(BF16)
    v_ref[0] = kv[:, d:].astype(BF16)


def _memkv(mem, g, wkv, layer, k_gain):
    b, m, d = mem.shape
    spec = pl.BlockSpec((1, m, d), lambda bi: (bi, 0, 0))
    return pl.pallas_call(
        _memkv_body,
        out_shape=(jax.ShapeDtypeStruct((b, d, m), BF16), jax.ShapeDtypeStruct((b, m, d), BF16)),
        grid=(b,),
        in_specs=[spec, _const_spec((1, d)), _layer_spec(wkv.shape[1:], layer), _const_spec(k_gain.shape)],
        out_specs=(pl.BlockSpec((1, d, m), lambda bi: (bi, 0, 0)), spec),
        compiler_params=_cparams(("parallel",)),
        name="memory_kv",
    )(mem, g, wkv, k_gain)


def _mix_out_body(x_ref, ya_ref, yc_ref, hf_ref, hb_ref, og_ref, mg_ref, bd_ref, wo_ref,
                  xg_ref, wq_ref, qg_ref, kt_ref, v_ref, wxo_ref, o_ref):
    hs = hf_ref[0] + hb_ref[0]
    hn = hs * lax.rsqrt(_group_mean(hs * hs, bd_ref[...], split=False) + EPS) * mg_ref[...]
    ym = (og_ref[0] * hn).astype(BF16)
    x = x_ref[0]
    x = (x + jnp.dot(ya_ref[0], wo_ref[0:ATTN_W, :], preferred_element_type=F32)
         + jnp.dot(yc_ref[0], wo_ref[ATTN_W:ATTN_W + CONV_CH, :], preferred_element_type=F32)
         + jnp.dot(ym, wo_ref[ATTN_W + CONV_CH:, :], preferred_element_type=F32))

    xb = _rms(x, xg_ref[...]).astype(BF16)
    qx = jnp.dot(xb, wq_ref[...], preferred_element_type=F32)
    hd = qx.shape[1] // XATTN_HEADS
    outs = []
    for h in range(XATTN_HEADS):
        sl = slice(h * hd, (h + 1) * hd)
        qh = (_rms(qx[:, sl], qg_ref[...]) * (hd ** -0.5)).astype(BF16)
        s = jnp.dot(qh, kt_ref[0, sl, :], preferred_element_type=F32)
        p = jnp.exp(s - jnp.max(s, axis=-1, keepdims=True))
        l = jnp.sum(p, axis=-1, keepdims=True)
        outs.append((jnp.dot(p.astype(BF16), v_ref[0, :, sl], preferred_element_type=F32) / l).astype(BF16))
    o = jnp.concatenate(outs, axis=1)
    o_ref[0] = x + jnp.dot(o, wxo_ref[...], preferred_element_type=F32)


def _mix_out(x, ya, yc, hf, hb, og, m_gain, bd, w_out, x_gain, wq, q_gain, mk, mv, wo, layer, *, tm):
    b, s, d = x.shape
    tok = lambda width: pl.BlockSpec((1, tm, width), lambda bi, i: (bi, i, 0))
    mem_spec = lambda a: pl.BlockSpec((1,) + a.shape[1:], lambda bi, i: (bi, 0, 0))
    return pl.pallas_call(
        _mix_out_body,
        out_shape=jax.ShapeDtypeStruct((b, s, d), F32),
        grid=(b, s // tm),
        in_specs=[tok(d), tok(ATTN_W), tok(CONV_CH), tok(MLSTM_W), tok(MLSTM_W), tok(MLSTM_W),
                  _const_spec(m_gain.shape), _const_spec(bd.shape), _layer_spec(w_out.shape[1:], layer),
                  _const_spec(x_gain.shape), _layer_spec(wq.shape[1:], layer), _const_spec(q_gain.shape),
                  mem_spec(mk), mem_spec(mv), _layer_spec(wo.shape[1:], layer)],
        out_specs=tok(d),
        compiler_params=_cparams(("parallel", "parallel")),
        name="mix_out_cross_attention",
    )(x, ya, yc, hf, hb, og, m_gain, bd, w_out, x_gain, wq, q_gain, mk, mv, wo)


def _rope_tables(s_len):
    rows = s_len // GRID_W
    row_idx = jnp.repeat(jnp.arange(rows, dtype=jnp.int32), GRID_W).astype(F32)
    col_idx = jnp.tile(jnp.arange(GRID_W, dtype=jnp.int32), rows).astype(F32)
    n_freq = HEAD_DIM // 4
    inv_freq = jnp.float32(ROPE_THETA) ** (-jnp.arange(n_freq, dtype=F32) / n_freq)
    ang = jnp.concatenate([row_idx[:, None] * inv_freq, col_idx[:, None] * inv_freq], axis=-1)
    cos = jnp.repeat(jnp.cos(ang), 2, axis=-1)
    sin = jnp.repeat(jnp.sin(ang), 2, axis=-1) * jnp.tile(jnp.array([-1.0, 1.0], F32), HEAD_DIM // 2)
    return jnp.tile(cos, (1, 2)), jnp.tile(sin, (1, 2))


def _block_diag(scale):
    idx = jnp.arange(MLSTM_W) // MLSTM_HEAD_DIM
    return jnp.where(idx[:, None] == idx[None, :], scale, 0.0).astype(BF16)


def _pick_tile(n, pref):
    t = min(n, pref)
    while n % t:
        t //= 2
    return t


def kernel(x, mem, ffn1_norm, ffn1_w13, ffn1_w2, mix_norm, w_in, attn_q_norm, attn_k_norm, conv_dw_w,
           conv_dw_b, conv_ln_g, conv_ln_b, mlstm_gate_b, mlstm_out_norm, w_out, xattn_norm, mem_norm,
           xattn_wq, xattn_wkv, xattn_q_norm, xattn_k_norm, xattn_wo, ffn2_norm, ffn2_w13, ffn2_w2):
    b, s, d = x.shape
    depth = w_in.shape[0]
    cos, sin = _rope_tables(s)
    bd_mean = _block_diag(1.0 / MLSTM_HEAD_DIM)
    bd_sum = _block_diag(1.0)

    tm_ffn = _pick_tile(b * s, 512)
    tm_tok = _pick_tile(s, 1024)
    tq = _pick_tile(s, 512)
    ts = _pick_tile(s, 512)
    row = lambda v: v.reshape(1, -1)

    bf = lambda w: w.astype(BF16)
    ffn1_w13, ffn1_w2, ffn2_w13, ffn2_w2 = bf(ffn1_w13), bf(ffn1_w2), bf(ffn2_w13), bf(ffn2_w2)
    w_out, xattn_wq, xattn_wkv, xattn_wo = bf(w_out), bf(xattn_wq), bf(xattn_wkv), bf(xattn_wo)
    w_gate = w_in[:, :, OFF_MG:]
    w_gate_pad = bf(jnp.pad(w_gate, ((0, 0), (0, 0), (0, LANES - N_GATES))))
    w_gate_t = bf(jnp.swapaxes(w_gate, 1, 2))
    w_in = bf(w_in)

    for l in range(depth):
        x = _ffn(x.reshape(b * s, d), row(ffn1_norm[l]), ffn1_w13, ffn1_w2, l,
                 tm=tm_ffn, n_chunks=FFN_CHUNKS).reshape(b, s, d)

        gate_b = mlstm_gate_b[l].reshape(1, N_GATES)
        q, kk, vt, u, mq, mk, mv, og, gate, gate_t = _inproj(
            x, row(mix_norm[l]), w_in, w_gate_pad, w_gate_t, l,
            jnp.pad(gate_b, ((0, 0), (0, LANES - N_GATES))), gate_b.reshape(N_GATES, 1),
            row(jnp.tile(attn_q_norm[l], ATTN_HEADS)), row(jnp.tile(attn_k_norm[l], ATTN_KV_HEADS)),
            cos, sin, bd_mean, tm=tm_tok)

        ya = _attention(q, kk, vt, tq=tq)
        yc = _conv(u, conv_dw_w[l], row(conv_dw_b[l]), row(conv_ln_g[l]), row(conv_ln_b[l]), ts=ts)
        hf, hb = _mlstm(mq, mk, mv, gate, gate_t, bd_sum)
        mem_k, mem_v = _memkv(mem, row(mem_norm[l]), xattn_wkv, l, row(xattn_k_norm[l]))
        x = _mix_out(x, ya, yc, hf, hb, og, row(mlstm_out_norm[l]), bd_mean, w_out,
                     row(xattn_norm[l]), xattn_wq, row(xattn_q_norm[l]), mem_k, mem_v, xattn_wo, l, tm=tm_tok)

        x = _ffn(x.reshape(b * s, d), row(ffn2_norm[l]), ffn2_w13, ffn2_w2, l,
                 tm=tm_ffn, n_chunks=FFN_CHUNKS).reshape(b, s, d)
    return x
```

```python
import functools

import jax
import jax.numpy as jnp
from jax import lax
from jax.experimental import pallas as pl
from jax.experimental.pallas import tpu as pltpu

F32 = jnp.float32
BF16 = jnp.bfloat16

GRID_W = 64
EPS = 1e-6
LOG2E = 1.4426950408889634
ATTN_HEADS = 8
ATTN_KV_HEADS = 2
HEAD_DIM = 64
ATTN_W = ATTN_HEADS * HEAD_DIM
KV_W = ATTN_KV_HEADS * HEAD_DIM
ROPE_THETA = 10000.0
CONV_CH = 256
CONV_WIDTH = 31
CONV_PAD = CONV_WIDTH // 2
MLSTM_HEADS = 4
MLSTM_HEAD_DIM = 64
MLSTM_W = MLSTM_HEADS * MLSTM_HEAD_DIM
MLSTM_CHUNK = 128
N_GATES = 16
XATTN_HEADS = 4
OFF_K = ATTN_W
OFF_V = OFF_K + KV_W
OFF_CONV = OFF_V + KV_W
OFF_MQ = OFF_CONV + 2 * CONV_CH
OFF_MK = OFF_MQ + MLSTM_W
OFF_MV = OFF_MK + MLSTM_W
OFF_MO = OFF_MV + MLSTM_W
OFF_MG = OFF_MO + MLSTM_W

LANES = 128
SUBLANES = 8
HALO_ROWS = 16
KEY_CHUNK = 512
FFN_CHUNKS = 2
BF16_ROWS = 16
VT_ROWS = HEAD_DIM + BF16_ROWS
VMEM_LIMIT = 56 * 1024 * 1024


def _cparams(sem):
    return pltpu.CompilerParams(dimension_semantics=sem, vmem_limit_bytes=VMEM_LIMIT)


def _const_spec(shape):
    nd = len(shape)
    return pl.BlockSpec(shape, lambda *_: (0,) * nd)


def _layer_spec(block2d, layer, **kw):
    return pl.BlockSpec((None,) + tuple(block2d), lambda *_: (layer, 0, 0), **kw)


def _rms(x, g):
    return x * lax.rsqrt(jnp.mean(x * x, axis=-1, keepdims=True) + EPS) * g


def _group_mean(sq, bd, split=True):
    hi = sq.astype(BF16)
    out = jnp.dot(hi, bd, preferred_element_type=F32)
    if split:
        lo = (sq - hi.astype(F32)).astype(BF16)
        out = out + jnp.dot(lo, bd, preferred_element_type=F32)
    return out


def _swap_pairs(x):
    nxt = pltpu.roll(x, LANES - 1, axis=1)
    prv = pltpu.roll(x, 1, axis=1)
    lane = lax.broadcasted_iota(jnp.int32, x.shape, 1)
    return jnp.where((lane & 1) == 0, nxt, prv)


def _sigmoid(x):
    return 1.0 / (1.0 + jnp.exp(-x))


def _ffn_body(x_ref, g_ref, w13_ref, w2_ref, o_ref, *, d_ff, n_chunks):
    x = x_ref[...]
    xb = _rms(x, g_ref[...]).astype(BF16)
    fc = d_ff // n_chunks
    acc = None
    for c in range(n_chunks):
        a = jnp.dot(xb, w13_ref[:, c * fc:(c + 1) * fc], preferred_element_type=F32)
        b = jnp.dot(xb, w13_ref[:, d_ff + c * fc:d_ff + (c + 1) * fc], preferred_element_type=F32)
        act = (a * _sigmoid(a) * b).astype(BF16)
        part = jnp.dot(act, w2_ref[c * fc:(c + 1) * fc, :], preferred_element_type=F32)
        acc = part if acc is None else acc + part
    o_ref[...] = x + 0.5 * acc


def _ffn(x2d, g, w13, w2, layer, *, tm, n_chunks):
    n, d = x2d.shape
    d_ff = w2.shape[1]
    return pl.pallas_call(
        functools.partial(_ffn_body, d_ff=d_ff, n_chunks=n_chunks),
        out_shape=jax.ShapeDtypeStruct((n, d), F32),
        grid=(n // tm,),
        in_specs=[
            pl.BlockSpec((tm, d), lambda i: (i, 0)),
            _const_spec((1, d)),
            _layer_spec((d, 2 * d_ff), layer, pipeline_mode=pl.Buffered(1)),
            _layer_spec((d_ff, d), layer, pipeline_mode=pl.Buffered(1)),
        ],
        out_specs=pl.BlockSpec((tm, d), lambda i: (i, 0)),
        compiler_params=_cparams(("parallel",)),
        name="swiglu_half_step",
    )(x2d, g, w13, w2)


def _inproj_body(x_ref, g_ref, w_ref, wg_ref, wgt_ref, gb_ref, gbt_ref, qg_ref, kg_ref,
                 cos_ref, sin_ref, bd_ref,
                 q_ref, k_ref, vt_ref, u_ref, mq_ref, mk_ref, mv_ref, og_ref, gate_ref, gatet_ref):
    xb = _rms(x_ref[0], g_ref[...]).astype(BF16)
    h = jnp.dot(xb, w_ref[...], preferred_element_type=F32)
    bd = bd_ref[...]
    cos = cos_ref[...]
    sin = sin_ref[...]

    hq = h[:, 0:ATTN_W]
    ms = jnp.concatenate([_group_mean(hq[:, 0:256] * hq[:, 0:256], bd, split=False),
                          _group_mean(hq[:, 256:512] * hq[:, 256:512], bd, split=False)], axis=1)
    qn = hq * lax.rsqrt(ms + EPS) * qg_ref[...]
    in_first = lax.broadcasted_iota(jnp.int32, cos.shape, 1) < HEAD_DIM
    group = ATTN_HEADS // ATTN_KV_HEADS
    for j in range(ATTN_W // LANES):
        z = qn[:, j * LANES:(j + 1) * LANES]
        r = (z * cos + _swap_pairs(z) * sin) * (HEAD_DIM ** -0.5 * LOG2E)
        rolled = pltpu.roll(r, HEAD_DIM, axis=1)
        for half in range(2):
            hd = 2 * j + half
            g = hd // group
            z = r if half == g else rolled
            keep = jnp.where(in_first, z, 0.0) if g == 0 else jnp.where(in_first, 0.0, z)
            q_ref[0, hd] = keep.astype(BF16)

    hk = h[:, OFF_K:OFF_V]
    msk = _group_mean(hk * hk, bd_ref[0:LANES, 0:LANES], split=False)
    kn = hk * lax.rsqrt(msk + EPS) * kg_ref[...]
    k_ref[0] = (kn * cos + _swap_pairs(kn) * sin).astype(BF16)
    hv = h[:, OFF_V:OFF_CONV]
    for r in range(hv.shape[0] // KEY_CHUNK):
        vt = hv[r * KEY_CHUNK:(r + 1) * KEY_CHUNK, :].T.astype(BF16)
        for g in range(ATTN_KV_HEADS):
            vt_ref[0, r, g, 0:HEAD_DIM] = vt[g * HEAD_DIM:(g + 1) * HEAD_DIM, :]
            vt_ref[0, r, g, HEAD_DIM:VT_ROWS] = jnp.ones((VT_ROWS - HEAD_DIM, KEY_CHUNK), BF16)

    u_ref[0] = h[:, OFF_CONV:OFF_CONV + CONV_CH] * _sigmoid(h[:, OFF_CONV + CONV_CH:OFF_MQ])

    mq_ref[0] = h[:, OFF_MQ:OFF_MK].astype(BF16)
    mk_ref[0] = (h[:, OFF_MK:OFF_MV] * (MLSTM_HEAD_DIM ** -0.5)).astype(BF16)
    mv_ref[0] = h[:, OFF_MV:OFF_MO].astype(BF16)
    og_ref[0] = _sigmoid(h[:, OFF_MO:OFF_MG])

    def log_gate(pre, is_forget):
        ls = jnp.minimum(pre, 0.0) - jnp.log1p(jnp.exp(-jnp.abs(pre)))
        return jnp.where(is_forget, ls, pre)

    gp = jnp.dot(xb, wg_ref[...], preferred_element_type=F32) + gb_ref[...]
    col = lax.broadcasted_iota(jnp.int32, gp.shape, 1)
    gate_ref[0] = log_gate(gp, ((col >> 2) & 1) == 1)
    gpt = lax.dot_general(wgt_ref[...], xb, (((1,), (1,)), ((), ())),
                          preferred_element_type=F32) + gbt_ref[...]
    row = lax.broadcasted_iota(jnp.int32, gpt.shape, 0)
    gatet_ref[0] = log_gate(gpt, ((row >> 2) & 1) == 1)


def _inproj(x, g, w_in, w_gate, w_gate_t, layer, gate_b, gate_bt, q_gain, k_gain, cos, sin, bd, *, tm):
    b, s, d = x.shape
    tok = lambda width, dt: jax.ShapeDtypeStruct((b, s, width), dt)
    tok_spec = lambda width: pl.BlockSpec((1, tm, width), lambda bi, i: (bi, i, 0))
    q_spec = pl.BlockSpec((1, ATTN_HEADS, tm, LANES), lambda bi, i: (bi, 0, i, 0))
    vt_shape = (b, s // KEY_CHUNK, ATTN_KV_HEADS, VT_ROWS, KEY_CHUNK)
    vt_spec = pl.BlockSpec((1, tm // KEY_CHUNK) + vt_shape[2:], lambda bi, i: (bi, i, 0, 0, 0))
    return pl.pallas_call(
        _inproj_body,
        out_shape=(
            jax.ShapeDtypeStruct((b, ATTN_HEADS, s, LANES), BF16),
            tok(KV_W, BF16),
            jax.ShapeDtypeStruct(vt_shape, BF16),
            tok(CONV_CH, F32),
            tok(MLSTM_W, BF16), tok(MLSTM_W, BF16), tok(MLSTM_W, BF16),
            tok(MLSTM_W, F32),
            tok(LANES, F32),
            jax.ShapeDtypeStruct((b, N_GATES, s), F32),
        ),
        grid=(b, s // tm),
        in_specs=[
            tok_spec(d),
            _const_spec((1, d)),
            _layer_spec((d, OFF_MG), layer),
            _layer_spec((d, LANES), layer),
            _layer_spec((N_GATES, d), layer),
            _const_spec((1, LANES)),
            _const_spec((N_GATES, 1)),
            _const_spec((1, ATTN_W)),
            _const_spec((1, LANES)),
            pl.BlockSpec((tm, LANES), lambda bi, i: (i, 0)),
            pl.BlockSpec((tm, LANES), lambda bi, i: (i, 0)),
            _const_spec(bd.shape),
        ],
        out_specs=(
            q_spec, tok_spec(KV_W), vt_spec, tok_spec(CONV_CH),
            tok_spec(MLSTM_W), tok_spec(MLSTM_W), tok_spec(MLSTM_W), tok_spec(MLSTM_W),
            tok_spec(LANES),
            pl.BlockSpec((1, N_GATES, tm), lambda bi, i: (bi, 0, i)),
        ),
        compiler_params=_cparams(("parallel", "parallel")),
        name="input_projection",
    )(x, g, w_in, w_gate, w_gate_t, gate_b, gate_bt, q_gain, k_gain, cos, sin, bd)


def _attn_body(q_ref, k_ref, vt_ref, o_ref, s_ref, mx_ref, qt_ref, *, unroll):
    tq = q_ref.shape[2]
    n_chunks = vt_ref.shape[1]
    tk = vt_ref.shape[4]

    for j in range(2):
        qt_ref[j] = q_ref[0, j].astype(F32).T.astype(BF16)

    def scores(c, slot):
        kc = k_ref[0, pl.ds(pl.multiple_of(c * tk, tk), tk), :]
        for j in range(2):
            s = jnp.dot(kc, qt_ref[j], preferred_element_type=F32)
            s_ref[slot, j] = s
            mx_ref[slot, j] = jnp.max(s, axis=0, keepdims=True)

    def consume(c, slot, state):
        vtc = vt_ref[0, c, 0]
        out = []
        for j in range(2):
            m, acc = state[j]
            s = s_ref[slot, j]
            m_new = jnp.maximum(m, mx_ref[slot, j])
            alpha = jnp.exp2(m - m_new)
            p = jnp.exp2(s - m_new).astype(BF16)
            out.append((m_new, alpha * acc + jnp.dot(vtc, p, preferred_element_type=F32)))
        return tuple(out)

    def step(i, state):
        for u in range(unroll):
            c = unroll * i + u
            scores(jnp.minimum(c + 1, n_chunks - 1), (u + 1) % 2)
            state = consume(c, u % 2, state)
        return state

    init = tuple((jnp.full((1, tq), -jnp.inf, F32), jnp.zeros((VT_ROWS, tq), F32)) for _ in range(2))
    scores(0, 0)
    (_, a0), (_, a1) = lax.fori_loop(0, n_chunks // unroll, step, init)
    o = [a[0:HEAD_DIM] / a[HEAD_DIM:HEAD_DIM + 1] for a in (a0, a1)]
    o_ref[0] = jnp.concatenate(o, axis=0).T.astype(BF16)


def _attention(q, k, vt, *, tq):
    b, _, s, _ = q.shape
    n_chunks, _, hd, tk = vt.shape[1:]
    unroll = max([u for u in (8, 4, 2) if n_chunks % u == 0 and n_chunks // u >= 2] or [2])
    assert n_chunks % unroll == 0
    heads_per_kv = ATTN_HEADS // ATTN_KV_HEADS
    return pl.pallas_call(
        functools.partial(_attn_body, unroll=unroll),
        out_shape=jax.ShapeDtypeStruct((b, s, ATTN_W), BF16),
        grid=(b, ATTN_HEADS // 2, s // tq),
        in_specs=[pl.BlockSpec((1, 2, tq, LANES), lambda bi, p, i: (bi, p, i, 0)),
                  pl.BlockSpec((1, s, KV_W), lambda bi, p, i: (bi, 0, 0)),
                  pl.BlockSpec((1, n_chunks, 1, hd, tk), lambda bi, p, i: (bi, 0, (2 * p) // heads_per_kv, 0, 0))],
        out_specs=pl.BlockSpec((1, tq, LANES), lambda bi, p, i: (bi, i, p)),
        scratch_shapes=[pltpu.VMEM((2, 2, tk, tq), F32), pltpu.VMEM((2, 2, 1, tq), F32),
                        pltpu.VMEM((2, LANES, tq), BF16)],
        compiler_params=_cparams(("parallel", "parallel", "arbitrary")),
        name="gqa_attention",
    )(q, k, vt)


def _conv_body(prev_ref, cur_ref, next_ref, w_ref, b_ref, lg_ref, lb_ref, o_ref, shift_ref):
    i = pl.program_id(1)
    ts = cur_ref.shape[1]
    prev = jnp.where(i > 0, prev_ref[0], 0.0)
    nxt = jnp.where(i < pl.num_programs(1) - 1, next_ref[0], 0.0)
    win = jnp.concatenate([prev, cur_ref[0], nxt], axis=0)
    w = w_ref[...]
    acc = jnp.zeros((ts, CONV_CH), F32) + b_ref[...]
    first_off = HALO_ROWS - CONV_PAD
    span = shift_ref.shape[0] - ts
    assert span >= ((first_off + CONV_WIDTH - 1) // SUBLANES) * SUBLANES
    assert SUBLANES - 1 + ts + span <= win.shape[0]
    for r in range(SUBLANES):
        shift_ref[...] = win[r:r + ts + span, :]
        for t in range(CONV_WIDTH):
            off = first_off + t
            if off % SUBLANES == r:
                a = off - r
                acc = acc + shift_ref[a:a + ts, :] * w[t:t + 1, :]
    mu = jnp.mean(acc, axis=-1, keepdims=True)
    cen = acc - mu
    var = jnp.mean(cen * cen, axis=-1, keepdims=True)
    un = cen * lax.rsqrt(var + EPS) * lg_ref[...] + lb_ref[...]
    o_ref[0] = (un * _sigmoid(un)).astype(BF16)


def _conv(u, dw_w, dw_b, ln_g, ln_b, *, ts):
    b, s, c = u.shape
    r = ts // HALO_ROWS
    n_halo = s // HALO_ROWS
    return pl.pallas_call(
        _conv_body,
        out_shape=jax.ShapeDtypeStruct((b, s, c), BF16),
        grid=(b, s // ts),
        in_specs=[
            pl.BlockSpec((1, HALO_ROWS, c), lambda bi, i: (bi, jnp.maximum(i * r - 1, 0), 0)),
            pl.BlockSpec((1, ts, c), lambda bi, i: (bi, i, 0)),
            pl.BlockSpec((1, HALO_ROWS, c), lambda bi, i: (bi, jnp.minimum((i + 1) * r, n_halo - 1), 0)),
            _const_spec((CONV_WIDTH, c)),
            _const_spec((1, c)), _const_spec((1, c)), _const_spec((1, c)),
        ],
        out_specs=pl.BlockSpec((1, ts, c), lambda bi, i: (bi, i, 0)),
        scratch_shapes=[pltpu.VMEM((ts + 2 * HALO_ROWS - SUBLANES, c), F32)],
        compiler_params=_cparams(("parallel", "parallel")),
        name="conformer_conv",
    )(u, u, u, dw_w, dw_b, ln_g, ln_b)


def _expand_heads(cols):
    rows = cols[0].shape[0]
    hid = lax.broadcasted_iota(jnp.int32, (rows, MLSTM_W), 1) >> 6
    out = jnp.broadcast_to(cols[3], (rows, MLSTM_W))
    for h in (2, 1, 0):
        out = jnp.where(hid == h, cols[h], out)
    return out


def _split3(x):
    a = x.astype(BF16)
    r = x - a.astype(F32)
    b = r.astype(BF16)
    c = (r - b.astype(F32)).astype(BF16)
    return a, b, c


def _mlstm_direction(d, q, k, v, gate, gate_t, c_ref, n_ref, m_ref, bd1):
    si = d
    L = q.shape[0]
    r_i = lax.broadcasted_iota(jnp.int32, (L, L), 0)
    c_i = lax.broadcasted_iota(jnp.int32, (L, L), 1)
    keep = (c_i <= r_i) if d == 0 else (c_i >= r_i)
    tri = jnp.where(keep, 1.0, 0.0).astype(BF16)
    tri_t = jnp.where((r_i <= c_i) if d == 0 else (r_i >= c_i), 1.0, 0.0).astype(BF16)

    cum_col = sum(jnp.dot(tri, t, preferred_element_type=F32) for t in _split3(gate))
    cum_row = sum(jnp.dot(t, tri_t, preferred_element_type=F32) for t in _split3(gate_t))
    tot = jnp.sum(gate, axis=0, keepdims=True)

    hid = lax.broadcasted_iota(jnp.int32, (L, MLSTM_W), 1) >> 6
    m_prev_l = m_ref[si]
    qf = q.astype(F32)
    kf = k.astype(F32)

    m_t_c, inter_c, rs_c, wend_c, mnew_c, decay_c = [], [], [], [], [], []
    num_intra = jnp.zeros((L, MLSTM_W), F32)
    for h in range(MLSTM_HEADS):
        ji = d * 8 + h
        jf = d * 8 + MLSTM_HEADS + h
        m_prev = m_prev_l[:, h * MLSTM_HEAD_DIM:h * MLSTM_HEAD_DIM + 1]
        b_col = cum_col[:, jf:jf + 1]
        i_col = gate[:, ji:ji + 1]
        b_row = cum_row[jf:jf + 1, :]
        i_row = gate_t[ji:ji + 1, :]
        b_end = tot[:, jf:jf + 1]

        logw = jnp.where(keep, b_col - b_row + i_row, -jnp.inf)
        m_inter = b_col + m_prev
        m_t = jnp.maximum(jnp.max(logw, axis=-1, keepdims=True), m_inter)
        w = jnp.exp(logw - m_t)
        qm = jnp.where(hid == h, qf, 0.0).astype(BF16)
        s = lax.dot_general(qm, k, (((1,), (1,)), ((), ())), preferred_element_type=F32) * w
        r = jnp.dot(s.astype(BF16), v, preferred_element_type=F32)
        num_intra = jnp.where(hid == h, r, num_intra)
        m_t_c.append(m_t)
        inter_c.append(jnp.exp(m_inter - m_t))
        rs_c.append(jnp.sum(s, axis=-1, keepdims=True))

        logw_end = b_end - b_col + i_col
        m_new = jnp.maximum(b_end + m_prev, jnp.max(logw_end, axis=0, keepdims=True))
        wend_c.append(jnp.exp(logw_end - m_new))
        mnew_c.append(m_new)
        decay_c.append(jnp.exp(b_end + m_prev - m_new))

    m_t_l = _expand_heads(m_t_c)
    inter_l = _expand_heads(inter_c)
    c_mat = c_ref[si]
    n_vec = n_ref[si]
    c_hi = c_mat.astype(BF16)
    c_lo = (c_mat - c_hi.astype(F32)).astype(BF16)
    qc = (jnp.dot(q, c_hi, preferred_element_type=F32)
          + jnp.dot(q, c_lo, preferred_element_type=F32))
    qn_l = _group_mean(qf * n_vec, bd1)
    num = num_intra + inter_l * qc
    den = _expand_heads(rs_c) + inter_l * qn_l
    h_out = num / jnp.maximum(jnp.abs(den), jnp.exp(-m_t_l))

    decay_l = _expand_heads(decay_c)
    kw = kf * _expand_heads(wend_c)
    kv = lax.dot_general(kw.astype(BF16), v, (((0,), (0,)), ((), ())),
                         preferred_element_type=F32)
    rr = lax.broadcasted_iota(jnp.int32, (MLSTM_W, MLSTM_W), 0) >> 6
    cc = lax.broadcasted_iota(jnp.int32, (MLSTM_W, MLSTM_W), 1) >> 6
    c_ref[si] = decay_l * c_mat + jnp.where(rr == cc, kv, 0.0)
    n_ref[si] = decay_l * n_vec + jnp.sum(kw, axis=0, keepdims=True)
    m_ref[si] = _expand_heads(mnew_c)
    return h_out


def _mlstm_body(qf_ref, kf_ref, vf_ref, gf_ref, gtf_ref, qb_ref, kb_ref, vb_ref, gb_ref, gtb_ref,
                bd1_ref, hf_ref, hb_ref, c_ref, n_ref, m_ref):
    @pl.when(pl.program_id(1) == 0)
    def _():
        c_ref[...] = jnp.zeros_like(c_ref)
        n_ref[...] = jnp.zeros_like(n_ref)
        m_ref[...] = jnp.zeros_like(m_ref)

    bd1 = bd1_ref[...]
    hf_ref[0] = _mlstm_direction(0, qf_ref[0], kf_ref[0], vf_ref[0], gf_ref[0], gtf_ref[0],
                                 c_ref, n_ref, m_ref, bd1)
    hb_ref[0] = _mlstm_direction(1, qb_ref[0], kb_ref[0], vb_ref[0], gb_ref[0], gtb_ref[0],
                                 c_ref, n_ref, m_ref, bd1)


def _mlstm(mq, mk, mv, gate, gate_t, bd1):
    b, s, w = mq.shape
    L = MLSTM_CHUNK
    nc = s // L
    fwd = pl.BlockSpec((1, L, w), lambda bi, c: (bi, c, 0))
    bwd = pl.BlockSpec((1, L, w), lambda bi, c: (bi, nc - 1 - c, 0))
    gfwd = pl.BlockSpec((1, L, LANES), lambda bi, c: (bi, c, 0))
    gbwd = pl.BlockSpec((1, L, LANES), lambda bi, c: (bi, nc - 1 - c, 0))
    gtfwd = pl.BlockSpec((1, N_GATES, L), lambda bi, c: (bi, 0, c))
    gtbwd = pl.BlockSpec((1, N_GATES, L), lambda bi, c: (bi, 0, nc - 1 - c))
    n_scan = 2
    return pl.pallas_call(
        _mlstm_body,
        out_shape=(jax.ShapeDtypeStruct((b, s, w), F32), jax.ShapeDtypeStruct((b, s, w), F32)),
        grid=(b, nc),
        in_specs=[fwd, fwd, fwd, gfwd, gtfwd, bwd, bwd, bwd, gbwd, gtbwd, _const_spec(bd1.shape)],
        out_specs=(fwd, bwd),
        scratch_shapes=[pltpu.VMEM((n_scan, w, w), F32), pltpu.VMEM((n_scan, 1, w), F32),
                        pltpu.VMEM((n_scan, 1, w), F32)],
        compiler_params=_cparams(("parallel", "arbitrary")),
        name="mlstm_scan",
    )(mq, mk, mv, gate, gate_t, mq, mk, mv, gate, gate_t, bd1)


def _memkv_body(mem_ref, g_ref, w_ref, kg_ref, k_ref, v_ref):
    mb = _rms(mem_ref[0], g_ref[...]).astype(BF16)
    kv = jnp.dot(mb, w_ref[...], preferred_element_type=F32)
    d = kv.shape[1] // 2
    hd = d // XATTN_HEADS
    for h in range(XATTN_HEADS):
        k_ref[0, h * hd:(h + 1) * hd, :] = _rms(kv[:, h * hd:(h + 1) * hd], kg_ref[...]).T.astype(BF16)
    v_ref[0] = kv[:, d:].astype(BF16)


def _memkv(mem, g, wkv, layer, k_gain):
    b, m, d = mem.shape
    spec = pl.BlockSpec((1, m, d), lambda bi: (bi, 0, 0))
    return pl.pallas_call(
        _memkv_body,
        out_shape=(jax.ShapeDtypeStruct((b, d, m), BF16), jax.ShapeDtypeStruct((b, m, d), BF16)),
        grid=(b,),
        in_specs=[spec, _const_spec((1, d)), _layer_spec(wkv.shape[1:], layer), _const_spec(k_gain.shape)],
        out_specs=(pl.BlockSpec((1, d, m), lambda bi: (bi, 0, 0)), spec),
        compiler_params=_cparams(("parallel",)),
        name="memory_kv",
    )(mem, g, wkv, k_gain)


def _mix_out_body(x_ref, ya_ref, yc_ref, hf_ref, hb_ref, og_ref, mg_ref, bd_ref, wo_ref,
                  xg_ref, wq_ref, qg_ref, kt_ref, v_ref, wxo_ref, o_ref):
    hs = hf_ref[0] + hb_ref[0]
    hn = hs * lax.rsqrt(_group_mean(hs * hs, bd_ref[...], split=False) + EPS) * mg_ref[...]
    ym = (og_ref[0] * hn).astype(BF16)
    x = x_ref[0]
    x = (x + jnp.dot(ya_ref[0], wo_ref[0:ATTN_W, :], preferred_element_type=F32)
         + jnp.dot(yc_ref[0], wo_ref[ATTN_W:ATTN_W + CONV_CH, :], preferred_element_type=F32)
         + jnp.dot(ym, wo_ref[ATTN_W + CONV_CH:, :], preferred_element_type=F32))

    xb = _rms(x, xg_ref[...]).astype(BF16)
    qx = jnp.dot(xb, wq_ref[...], preferred_element_type=F32)
    hd = qx.shape[1] // XATTN_HEADS
    outs = []
    for h in range(XATTN_HEADS):
        sl = slice(h * hd, (h + 1) * hd)
        qh = (_rms(qx[:, sl], qg_ref[...]) * (hd ** -0.5)).astype(BF16)
        s = jnp.dot(qh, kt_ref[0, sl, :], preferred_element_type=F32)
        p = jnp.exp(s - jnp.max(s, axis=-1, keepdims=True))
        l = jnp.sum(p, axis=-1, keepdims=True)
        outs.append((jnp.dot(p.astype(BF16), v_ref[0, :, sl], preferred_element_type=F32) / l).astype(BF16))
    o = jnp.concatenate(outs, axis=1)
    o_ref[0] = x + jnp.dot(o, wxo_ref[...], preferred_element_type=F32)


def _mix_out(x, ya, yc, hf, hb, og, m_gain, bd, w_out, x_gain, wq, q_gain, mk, mv, wo, layer, *, tm):
    b, s, d = x.shape
    tok = lambda width: pl.BlockSpec((1, tm, width), lambda bi, i: (bi, i, 0))
    mem_spec = lambda a: pl.BlockSpec((1,) + a.shape[1:], lambda bi, i: (bi, 0, 0))
    return pl.pallas_call(
        _mix_out_body,
        out_shape=jax.ShapeDtypeStruct((b, s, d), F32),
        grid=(b, s // tm),
        in_specs=[tok(d), tok(ATTN_W), tok(CONV_CH), tok(MLSTM_W), tok(MLSTM_W), tok(MLSTM_W),
                  _const_spec(m_gain.shape), _const_spec(bd.shape), _layer_spec(w_out.shape[1:], layer),
                  _const_spec(x_gain.shape), _layer_spec(wq.shape[1:], layer), _const_spec(q_gain.shape),
                  mem_spec(mk), mem_spec(mv), _layer_spec(wo.shape[1:], layer)],
        out_specs=tok(d),
        compiler_params=_cparams(("parallel", "parallel")),
        name="mix_out_cross_attention",
    )(x, ya, yc, hf, hb, og, m_gain, bd, w_out, x_gain, wq, q_gain, mk, mv, wo)


def _rope_tables(s_len):
    rows = s_len // GRID_W
    row_idx = jnp.repeat(jnp.arange(rows, dtype=jnp.int32), GRID_W).astype(F32)
    col_idx = jnp.tile(jnp.arange(GRID_W, dtype=jnp.int32), rows).astype(F32)
    n_freq = HEAD_DIM // 4
    inv_freq = jnp.float32(ROPE_THETA) ** (-jnp.arange(n_freq, dtype=F32) / n_freq)
    ang = jnp.concatenate([row_idx[:, None] * inv_freq, col_idx[:, None] * inv_freq], axis=-1)
    cos = jnp.repeat(jnp.cos(ang), 2, axis=-1)
    sin = jnp.repeat(jnp.sin(ang), 2, axis=-1) * jnp.tile(jnp.array([-1.0, 1.0], F32), HEAD_DIM // 2)
    return jnp.tile(cos, (1, 2)), jnp.tile(sin, (1, 2))


def _block_diag(scale):
    idx = jnp.arange(MLSTM_W) // MLSTM_HEAD_DIM
    return jnp.where(idx[:, None] == idx[None, :], scale, 0.0).astype(BF16)


def _pick_tile(n, pref):
    t = min(n, pref)
    while n % t:
        t //= 2
    return t


def kernel(x, mem, ffn1_norm, ffn1_w13, ffn1_w2, mix_norm, w_in, attn_q_norm, attn_k_norm, conv_dw_w,
           conv_dw_b, conv_ln_g, conv_ln_b, mlstm_gate_b, mlstm_out_norm, w_out, xattn_norm, mem_norm,
           xattn_wq, xattn_wkv, xattn_q_norm, xattn_k_norm, xattn_wo, ffn2_norm, ffn2_w13, ffn2_w2):
    b, s, d = x.shape
    depth = w_in.shape[0]
    cos, sin = _rope_tables(s)
    bd_mean = _block_diag(1.0 / MLSTM_HEAD_DIM)
    bd_sum = _block_diag(1.0)

    tm_ffn = _pick_tile(b * s, 1024)
    tm_tok = _pick_tile(s, 1024)
    tq = _pick_tile(s, 512)
    ts = _pick_tile(s, 512)
    row = lambda v: v.reshape(1, -1)

    bf = lambda w: w.astype(BF16)
    ffn1_w13, ffn1_w2, ffn2_w13, ffn2_w2 = bf(ffn1_w13), bf(ffn1_w2), bf(ffn2_w13), bf(ffn2_w2)
    w_out, xattn_wq, xattn_wkv, xattn_wo = bf(w_out), bf(xattn_wq), bf(xattn_wkv), bf(xattn_wo)
    w_gate = w_in[:, :, OFF_MG:]
    w_gate_pad = bf(jnp.pad(w_gate, ((0, 0), (0, 0), (0, LANES - N_GATES))))
    w_gate_t = bf(jnp.swapaxes(w_gate, 1, 2))
    w_in = bf(w_in)

    for l in range(depth):
        x = _ffn(x.reshape(b * s, d), row(ffn1_norm[l]), ffn1_w13, ffn1_w2, l,
                 tm=tm_ffn, n_chunks=FFN_CHUNKS).reshape(b, s, d)

        gate_b = mlstm_gate_b[l].reshape(1, N_GATES)
        q, kk, vt, u, mq, mk, mv, og, gate, gate_t = _inproj(
            x, row(mix_norm[l]), w_in, w_gate_pad, w_gate_t, l,
            jnp.pad(gate_b, ((0, 0), (0, LANES - N_GATES))), gate_b.reshape(N_GATES, 1),
            row(jnp.tile(attn_q_norm[l], ATTN_HEADS)), row(jnp.tile(attn_k_norm[l], ATTN_KV_HEADS)),
            cos, sin, bd_mean, tm=tm_tok)

        ya = _attention(q, kk, vt, tq=tq)
        yc = _conv(u, conv_dw_w[l], row(conv_dw_b[l]), row(conv_ln_g[l]), row(conv_ln_b[l]), ts=ts)
        hf, hb = _mlstm(mq, mk, mv, gate, gate_t, bd_sum)
        mem_k, mem_v = _memkv(mem, row(mem_norm[l]), xattn_wkv, l, row(xattn_k_norm[l]))
        x = _mix_out(x, ya, yc, hf, hb, og, row(mlstm_out_norm[l]), bd_mean, w_out,
                     row(xattn_norm[l]), xattn_wq, row(xattn_q_norm[l]), mem_k, mem_v, xattn_wo, l, tm=tm_tok)

        x = _ffn(x.reshape(b * s, d), row(ffn2_norm[l]), ffn2_w13, ffn2_w2, l,
                 tm=tm_ffn, n_chunks=FFN_CHUNKS).reshape(b, s, d)
    return x
```

```python
import functools

import jax
import jax.numpy as jnp
from jax import lax
from jax.experimental import pallas as pl
from jax.experimental.pallas import tpu as pltpu

F32 = jnp.float32
BF16 = jnp.bfloat16

GRID_W = 64
EPS = 1e-6
LOG2E = 1.4426950408889634
ATTN_HEADS = 8
ATTN_KV_HEADS = 2
HEAD_DIM = 64
ATTN_W = ATTN_HEADS * HEAD_DIM
KV_W = ATTN_KV_HEADS * HEAD_DIM
ROPE_THETA = 10000.0
CONV_CH = 256
CONV_WIDTH = 31
CONV_PAD = CONV_WIDTH // 2
MLSTM_HEADS = 4
MLSTM_HEAD_DIM = 64
MLSTM_W = MLSTM_HEADS * MLSTM_HEAD_DIM
MLSTM_CHUNK = 128
N_GATES = 16
XATTN_HEADS = 4
OFF_K = ATTN_W
OFF_V = OFF_K + KV_W
OFF_CONV = OFF_V + KV_W
OFF_MQ = OFF_CONV + 2 * CONV_CH
OFF_MK = OFF_MQ + MLSTM_W
OFF_MV = OFF_MK + MLSTM_W
OFF_MO = OFF_MV + MLSTM_W
OFF_MG = OFF_MO + MLSTM_W

LANES = 128
SUBLANES = 8
HALO_ROWS = 16
KEY_CHUNK = 512
FFN_CHUNKS = 1
BF16_ROWS = 16
VT_ROWS = HEAD_DIM + BF16_ROWS
VMEM_LIMIT = 56 * 1024 * 1024


def _cparams(sem):
    return pltpu.CompilerParams(dimension_semantics=sem, vmem_limit_bytes=VMEM_LIMIT)


def _const_spec(shape):
    nd = len(shape)
    return pl.BlockSpec(shape, lambda *_: (0,) * nd)


def _layer_spec(block2d, layer, **kw):
    return pl.BlockSpec((None,) + tuple(block2d), lambda *_: (layer, 0, 0), **kw)


def _rms(x, g):
    return x * lax.rsqrt(jnp.mean(x * x, axis=-1, keepdims=True) + EPS) * g


def _group_mean(sq, bd, split=True):
    hi = sq.astype(BF16)
    out = jnp.dot(hi, bd, preferred_element_type=F32)
    if split:
        lo = (sq - hi.astype(F32)).astype(BF16)
        out = out + jnp.dot(lo, bd, preferred_element_type=F32)
    return out


def _swap_pairs(x):
    nxt = pltpu.roll(x, LANES - 1, axis=1)
    prv = pltpu.roll(x, 1, axis=1)
    lane = lax.broadcasted_iota(jnp.int32, x.shape, 1)
    return jnp.where((lane & 1) == 0, nxt, prv)


def _sigmoid(x):
    return 1.0 / (1.0 + jnp.exp(-x))


def _ffn_body(x_ref, g_ref, w13_ref, w2_ref, o_ref, *, d_ff, n_chunks):
    x = x_ref[...]
    xb = _rms(x, g_ref[...]).astype(BF16)
    fc = d_ff // n_chunks
    acc = None
    for c in range(n_chunks):
        a = jnp.dot(xb, w13_ref[:, c * fc:(c + 1) * fc], preferred_element_type=F32)
        b = jnp.dot(xb, w13_ref[:, d_ff + c * fc:d_ff + (c + 1) * fc], preferred_element_type=F32)
        act = (a * _sigmoid(a) * b).astype(BF16)
        part = jnp.dot(act, w2_ref[c * fc:(c + 1) * fc, :], preferred_element_type=F32)
        acc = part if acc is None else acc + part
    o_ref[...] = x + 0.5 * acc


def _ffn(x2d, g, w13, w2, layer, *, tm, n_chunks):
    n, d = x2d.shape
    d_ff = w2.shape[1]
    return pl.pallas_call(
        functools.partial(_ffn_body, d_ff=d_ff, n_chunks=n_chunks),
        out_shape=jax.ShapeDtypeStruct((n, d), F32),
        grid=(n // tm,),
        in_specs=[
            pl.BlockSpec((tm, d), lambda i: (i, 0)),
            _const_spec((1, d)),
            _layer_spec((d, 2 * d_ff), layer, pipeline_mode=pl.Buffered(1)),
            _layer_spec((d_ff, d), layer, pipeline_mode=pl.Buffered(1)),
        ],
        out_specs=pl.BlockSpec((tm, d), lambda i: (i, 0)),
        compiler_params=_cparams(("parallel",)),
        name="swiglu_half_step",
    )(x2d, g, w13, w2)


def _inproj_body(x_ref, g_ref, w_ref, wg_ref, wgt_ref, gb_ref, gbt_ref, qg_ref, kg_ref,
                 cos_ref, sin_ref, bd_ref,
                 q_ref, k_ref, vt_ref, u_ref, mq_ref, mk_ref, mv_ref, og_ref, gate_ref, gatet_ref):
    xb = _rms(x_ref[0], g_ref[...]).astype(BF16)
    h = jnp.dot(xb, w_ref[...], preferred_element_type=F32)
    bd = bd_ref[...]
    cos = cos_ref[...]
    sin = sin_ref[...]

    hq = h[:, 0:ATTN_W]
    ms = jnp.concatenate([_group_mean(hq[:, 0:256] * hq[:, 0:256], bd, split=False),
                          _group_mean(hq[:, 256:512] * hq[:, 256:512], bd, split=False)], axis=1)
    qn = hq * lax.rsqrt(ms + EPS) * qg_ref[...]
    in_first = lax.broadcasted_iota(jnp.int32, cos.shape, 1) < HEAD_DIM
    group = ATTN_HEADS // ATTN_KV_HEADS
    for j in range(ATTN_W // LANES):
        z = qn[:, j * LANES:(j + 1) * LANES]
        r = (z * cos + _swap_pairs(z) * sin) * (HEAD_DIM ** -0.5 * LOG2E)
        rolled = pltpu.roll(r, HEAD_DIM, axis=1)
        for half in range(2):
            hd = 2 * j + half
            g = hd // group
            z = r if half == g else rolled
            keep = jnp.where(in_first, z, 0.0) if g == 0 else jnp.where(in_first, 0.0, z)
            q_ref[0, hd] = keep.astype(BF16)

    hk = h[:, OFF_K:OFF_V]
    msk = _group_mean(hk * hk, bd_ref[0:LANES, 0:LANES], split=False)
    kn = hk * lax.rsqrt(msk + EPS) * kg_ref[...]
    k_ref[0] = (kn * cos + _swap_pairs(kn) * sin).astype(BF16)
    hv = h[:, OFF_V:OFF_CONV]
    for r in range(hv.shape[0] // KEY_CHUNK):
        vt = hv[r * KEY_CHUNK:(r + 1) * KEY_CHUNK, :].T.astype(BF16)
        for g in range(ATTN_KV_HEADS):
            vt_ref[0, r, g, 0:HEAD_DIM] = vt[g * HEAD_DIM:(g + 1) * HEAD_DIM, :]
            vt_ref[0, r, g, HEAD_DIM:VT_ROWS] = jnp.ones((VT_ROWS - HEAD_DIM, KEY_CHUNK), BF16)

    u_ref[0] = h[:, OFF_CONV:OFF_CONV + CONV_CH] * _sigmoid(h[:, OFF_CONV + CONV_CH:OFF_MQ])

    mq_ref[0] = h[:, OFF_MQ:OFF_MK].astype(BF16)
    mk_ref[0] = (h[:, OFF_MK:OFF_MV] * (MLSTM_HEAD_DIM ** -0.5)).astype(BF16)
    mv_ref[0] = h[:, OFF_MV:OFF_MO].astype(BF16)
    og_ref[0] = _sigmoid(h[:, OFF_MO:OFF_MG])

    def log_gate(pre, is_forget):
        ls = jnp.minimum(pre, 0.0) - jnp.log1p(jnp.exp(-jnp.abs(pre)))
        return jnp.where(is_forget, ls, pre)

    gp = jnp.dot(xb, wg_ref[...], preferred_element_type=F32) + gb_ref[...]
    col = lax.broadcasted_iota(jnp.int32, gp.shape, 1)
    gate_ref[0] = log_gate(gp, ((col >> 2) & 1) == 1)
    gpt = lax.dot_general(wgt_ref[...], xb, (((1,), (1,)), ((), ())),
                          preferred_element_type=F32) + gbt_ref[...]
    row = lax.broadcasted_iota(jnp.int32, gpt.shape, 0)
    gatet_ref[0] = log_gate(gpt, ((row >> 2) & 1) == 1)


def _inproj(x, g, w_in, w_gate, w_gate_t, layer, gate_b, gate_bt, q_gain, k_gain, cos, sin, bd, *, tm):
    b, s, d = x.shape
    tok = lambda width, dt: jax.ShapeDtypeStruct((b, s, width), dt)
    tok_spec = lambda width: pl.BlockSpec((1, tm, width), lambda bi, i: (bi, i, 0))
    q_spec = pl.BlockSpec((1, ATTN_HEADS, tm, LANES), lambda bi, i: (bi, 0, i, 0))
    vt_shape = (b, s // KEY_CHUNK, ATTN_KV_HEADS, VT_ROWS, KEY_CHUNK)
    vt_spec = pl.BlockSpec((1, tm // KEY_CHUNK) + vt_shape[2:], lambda bi, i: (bi, i, 0, 0, 0))
    return pl.pallas_call(
        _inproj_body,
        out_shape=(
            jax.ShapeDtypeStruct((b, ATTN_HEADS, s, LANES), BF16),
            tok(KV_W, BF16),
            jax.ShapeDtypeStruct(vt_shape, BF16),
            tok(CONV_CH, F32),
            tok(MLSTM_W, BF16), tok(MLSTM_W, BF16), tok(MLSTM_W, BF16),
            tok(MLSTM_W, F32),
            tok(LANES, F32),
            jax.ShapeDtypeStruct((b, N_GATES, s), F32),
        ),
        grid=(b, s // tm),
        in_specs=[
            tok_spec(d),
            _const_spec((1, d)),
            _layer_spec((d, OFF_MG), layer),
            _layer_spec((d, LANES), layer),
            _layer_spec((N_GATES, d), layer),
            _const_spec((1, LANES)),
            _const_spec((N_GATES, 1)),
            _const_spec((1, ATTN_W)),
            _const_spec((1, LANES)),
            pl.BlockSpec((tm, LANES), lambda bi, i: (i, 0)),
            pl.BlockSpec((tm, LANES), lambda bi, i: (i, 0)),
            _const_spec(bd.shape),
        ],
        out_specs=(
            q_spec, tok_spec(KV_W), vt_spec, tok_spec(CONV_CH),
            tok_spec(MLSTM_W), tok_spec(MLSTM_W), tok_spec(MLSTM_W), tok_spec(MLSTM_W),
            tok_spec(LANES),
            pl.BlockSpec((1, N_GATES, tm), lambda bi, i: (bi, 0, i)),
        ),
        compiler_params=_cparams(("parallel", "parallel")),
        name="input_projection",
    )(x, g, w_in, w_gate, w_gate_t, gate_b, gate_bt, q_gain, k_gain, cos, sin, bd)


def _attn_body(q_ref, k_ref, vt_ref, o_ref, s_ref, mx_ref, qt_ref, *, unroll):
    tq = q_ref.shape[2]
    n_chunks = vt_ref.shape[1]
    tk = vt_ref.shape[4]

    for j in range(2):
        qt_ref[j] = q_ref[0, j].astype(F32).T.astype(BF16)

    def scores(c, slot):
        kc = k_ref[0, pl.ds(pl.multiple_of(c * tk, tk), tk), :]
        for j in range(2):
            s = jnp.dot(kc, qt_ref[j], preferred_element_type=F32)
            s_ref[slot, j] = s
            mx_ref[slot, j] = jnp.max(s, axis=0, keepdims=True)

    def consume(c, slot, state):
        vtc = vt_ref[0, c, 0]
        out = []
        for j in range(2):
            m, acc = state[j]
            s = s_ref[slot, j]
            m_new = jnp.maximum(m, mx_ref[slot, j])
            alpha = jnp.exp2(m - m_new)
            p = jnp.exp2(s - m_new).astype(BF16)
            out.append((m_new, alpha * acc + jnp.dot(vtc, p, preferred_element_type=F32)))
        return tuple(out)

    def step(i, state):
        for u in range(unroll):
            c = unroll * i + u
            scores(jnp.minimum(c + 1, n_chunks - 1), (u + 1) % 2)
            state = consume(c, u % 2, state)
        return state

    init = tuple((jnp.full((1, tq), -jnp.inf, F32), jnp.zeros((VT_ROWS, tq), F32)) for _ in range(2))
    scores(0, 0)
    (_, a0), (_, a1) = lax.fori_loop(0, n_chunks // unroll, step, init)
    o = [a[0:HEAD_DIM] / a[HEAD_DIM:HEAD_DIM + 1] for a in (a0, a1)]
    o_ref[0] = jnp.concatenate(o, axis=0).T.astype(BF16)


def _attention(q, k, vt, *, tq):
    b, _, s, _ = q.shape
    n_chunks, _, hd, tk = vt.shape[1:]
    unroll = max([u for u in (8, 4, 2) if n_chunks % u == 0 and n_chunks // u >= 2] or [2])
    assert n_chunks % unroll == 0
    heads_per_kv = ATTN_HEADS // ATTN_KV_HEADS
    return pl.pallas_call(
        functools.partial(_attn_body, unroll=unroll),
        out_shape=jax.ShapeDtypeStruct((b, s, ATTN_W), BF16),
        grid=(b, ATTN_HEADS // 2, s // tq),
        in_specs=[pl.BlockSpec((1, 2, tq, LANES), lambda bi, p, i: (bi, p, i, 0)),
                  pl.BlockSpec((1, s, KV_W), lambda bi, p, i: (bi, 0, 0)),
                  pl.BlockSpec((1, n_chunks, 1, hd, tk), lambda bi, p, i: (bi, 0, (2 * p) // heads_per_kv, 0, 0))],
        out_specs=pl.BlockSpec((1, tq, LANES), lambda bi, p, i: (bi, i, p)),
        scratch_shapes=[pltpu.VMEM((2, 2, tk, tq), F32), pltpu.VMEM((2, 2, 1, tq), F32),
                        pltpu.VMEM((2, LANES, tq), BF16)],
        compiler_params=_cparams(("parallel", "parallel", "arbitrary")),
        name="gqa_attention",
    )(q, k, vt)


def _conv_body(prev_ref, cur_ref, next_ref, w_ref, b_ref, lg_ref, lb_ref, o_ref, shift_ref):
    i = pl.program_id(1)
    ts = cur_ref.shape[1]
    prev = jnp.where(i > 0, prev_ref[0], 0.0)
    nxt = jnp.where(i < pl.num_programs(1) - 1, next_ref[0], 0.0)
    win = jnp.concatenate([prev, cur_ref[0], nxt], axis=0)
    w = w_ref[...]
    acc = jnp.zeros((ts, CONV_CH), F32) + b_ref[...]
    first_off = HALO_ROWS - CONV_PAD
    span = shift_ref.shape[0] - ts
    assert span >= ((first_off + CONV_WIDTH - 1) // SUBLANES) * SUBLANES
    assert SUBLANES - 1 + ts + span <= win.shape[0]
    for r in range(SUBLANES):
        shift_ref[...] = win[r:r + ts + span, :]
        for t in range(CONV_WIDTH):
            off = first_off + t
            if off % SUBLANES == r:
                a = off - r
                acc = acc + shift_ref[a:a + ts, :] * w[t:t + 1, :]
    mu = jnp.mean(acc, axis=-1, keepdims=True)
    cen = acc - mu
    var = jnp.mean(cen * cen, axis=-1, keepdims=True)
    un = cen * lax.rsqrt(var + EPS) * lg_ref[...] + lb_ref[...]
    o_ref[0] = (un * _sigmoid(un)).astype(BF16)


def _conv(u, dw_w, dw_b, ln_g, ln_b, *, ts):
    b, s, c = u.shape
    r = ts // HALO_ROWS
    n_halo = s // HALO_ROWS
    return pl.pallas_call(
        _conv_body,
        out_shape=jax.ShapeDtypeStruct((b, s, c), BF16),
        grid=(b, s // ts),
        in_specs=[
            pl.BlockSpec((1, HALO_ROWS, c), lambda bi, i: (bi, jnp.maximum(i * r - 1, 0), 0)),
            pl.BlockSpec((1, ts, c), lambda bi, i: (bi, i, 0)),
            pl.BlockSpec((1, HALO_ROWS, c), lambda bi, i: (bi, jnp.minimum((i + 1) * r, n_halo - 1), 0)),
            _const_spec((CONV_WIDTH, c)),
            _const_spec((1, c)), _const_spec((1, c)), _const_spec((1, c)),
        ],
        out_specs=pl.BlockSpec((1, ts, c), lambda bi, i: (bi, i, 0)),
        scratch_shapes=[pltpu.VMEM((ts + 2 * HALO_ROWS - SUBLANES, c), F32)],
        compiler_params=_cparams(("parallel", "parallel")),
        name="conformer_conv",
    )(u, u, u, dw_w, dw_b, ln_g, ln_b)


def _expand_heads(cols):
    rows = cols[0].shape[0]
    hid = lax.broadcasted_iota(jnp.int32, (rows, MLSTM_W), 1) >> 6
    out = jnp.broadcast_to(cols[3], (rows, MLSTM_W))
    for h in (2, 1, 0):
        out = jnp.where(hid == h, cols[h], out)
    return out


def _split3(x):
    a = x.astype(BF16)
    r = x - a.astype(F32)
    b = r.astype(BF16)
    c = (r - b.astype(F32)).astype(BF16)
    return a, b, c


def _mlstm_direction(d, q, k, v, gate, gate_t, c_ref, n_ref, m_ref, bd1):
    si = d
    L = q.shape[0]
    r_i = lax.broadcasted_iota(jnp.int32, (L, L), 0)
    c_i = lax.broadcasted_iota(jnp.int32, (L, L), 1)
    keep = (c_i <= r_i) if d == 0 else (c_i >= r_i)
    tri = jnp.where(keep, 1.0, 0.0).astype(BF16)
    tri_t = jnp.where((r_i <= c_i) if d == 0 else (r_i >= c_i), 1.0, 0.0).astype(BF16)

    cum_col = sum(jnp.dot(tri, t, preferred_element_type=F32) for t in _split3(gate))
    cum_row = sum(jnp.dot(t, tri_t, preferred_element_type=F32) for t in _split3(gate_t))
    tot = jnp.sum(gate, axis=0, keepdims=True)

    hid = lax.broadcasted_iota(jnp.int32, (L, MLSTM_W), 1) >> 6
    m_prev_l = m_ref[si]
    qf = q.astype(F32)
    kf = k.astype(F32)

    m_t_c, inter_c, rs_c, wend_c, mnew_c, decay_c = [], [], [], [], [], []
    num_intra = jnp.zeros((L, MLSTM_W), F32)
    for h in range(MLSTM_HEADS):
        ji = d * 8 + h
        jf = d * 8 + MLSTM_HEADS + h
        m_prev = m_prev_l[:, h * MLSTM_HEAD_DIM:h * MLSTM_HEAD_DIM + 1]
        b_col = cum_col[:, jf:jf + 1]
        i_col = gate[:, ji:ji + 1]
        b_row = cum_row[jf:jf + 1, :]
        i_row = gate_t[ji:ji + 1, :]
        b_end = tot[:, jf:jf + 1]

        logw = jnp.where(keep, b_col - b_row + i_row, -jnp.inf)
        m_inter = b_col + m_prev
        m_t = jnp.maximum(jnp.max(logw, axis=-1, keepdims=True), m_inter)
        w = jnp.exp(logw - m_t)
        qm = jnp.where(hid == h, qf, 0.0).astype(BF16)
        s = lax.dot_general(qm, k, (((1,), (1,)), ((), ())), preferred_element_type=F32) * w
        r = jnp.dot(s.astype(BF16), v, preferred_element_type=F32)
        num_intra = jnp.where(hid == h, r, num_intra)
        m_t_c.append(m_t)
        inter_c.append(jnp.exp(m_inter - m_t))
        rs_c.append(jnp.sum(s, axis=-1, keepdims=True))

        logw_end = b_end - b_col + i_col
        m_new = jnp.maximum(b_end + m_prev, jnp.max(logw_end, axis=0, keepdims=True))
        wend_c.append(jnp.exp(logw_end - m_new))
        mnew_c.append(m_new)
        decay_c.append(jnp.exp(b_end + m_prev - m_new))

    m_t_l = _expand_heads(m_t_c)
    inter_l = _expand_heads(inter_c)
    c_mat = c_ref[si]
    n_vec = n_ref[si]
    c_hi = c_mat.astype(BF16)
    c_lo = (c_mat - c_hi.astype(F32)).astype(BF16)
    qc = (jnp.dot(q, c_hi, preferred_element_type=F32)
          + jnp.dot(q, c_lo, preferred_element_type=F32))
    qn_l = _group_mean(qf * n_vec, bd1)
    num = num_intra + inter_l * qc
    den = _expand_heads(rs_c) + inter_l * qn_l
    h_out = num / jnp.maximum(jnp.abs(den), jnp.exp(-m_t_l))

    decay_l = _expand_heads(decay_c)
    kw = kf * _expand_heads(wend_c)
    kv = lax.dot_general(kw.astype(BF16), v, (((0,), (0,)), ((), ())),
                         preferred_element_type=F32)
    rr = lax.broadcasted_iota(jnp.int32, (MLSTM_W, MLSTM_W), 0) >> 6
    cc = lax.broadcasted_iota(jnp.int32, (MLSTM_W, MLSTM_W), 1) >> 6
    c_ref[si] = decay_l * c_mat + jnp.where(rr == cc, kv, 0.0)
    n_ref[si] = decay_l * n_vec + jnp.sum(kw, axis=0, keepdims=True)
    m_ref[si] = _expand_heads(mnew_c)
    return h_out


def _mlstm_body(qf_ref, kf_ref, vf_ref, gf_ref, gtf_ref, qb_ref, kb_ref, vb_ref, gb_ref, gtb_ref,
                bd1_ref, hf_ref, hb_ref, c_ref, n_ref, m_ref):
    @pl.when(pl.program_id(1) == 0)
    def _():
        c_ref[...] = jnp.zeros_like(c_ref)
        n_ref[...] = jnp.zeros_like(n_ref)
        m_ref[...] = jnp.zeros_like(m_ref)

    bd1 = bd1_ref[...]
    hf_ref[0] = _mlstm_direction(0, qf_ref[0], kf_ref[0], vf_ref[0], gf_ref[0], gtf_ref[0],
                                 c_ref, n_ref, m_ref, bd1)
    hb_ref[0] = _mlstm_direction(1, qb_ref[0], kb_ref[0], vb_ref[0], gb_ref[0], gtb_ref[0],
                                 c_ref, n_ref, m_ref, bd1)


def _mlstm(mq, mk, mv, gate, gate_t, bd1):
    b, s, w = mq.shape
    L = MLSTM_CHUNK
    nc = s // L
    fwd = pl.BlockSpec((1, L, w), lambda bi, c: (bi, c, 0))
    bwd = pl.BlockSpec((1, L, w), lambda bi, c: (bi, nc - 1 - c, 0))
    gfwd = pl.BlockSpec((1, L, LANES), lambda bi, c: (bi, c, 0))
    gbwd = pl.BlockSpec((1, L, LANES), lambda bi, c: (bi, nc - 1 - c, 0))
    gtfwd = pl.BlockSpec((1, N_GATES, L), lambda bi, c: (bi, 0, c))
    gtbwd = pl.BlockSpec((1, N_GATES, L), lambda bi, c: (bi, 0, nc - 1 - c))
    n_scan = 2
    return pl.pallas_call(
        _mlstm_body,
        out_shape=(jax.ShapeDtypeStruct((b, s, w), F32), jax.ShapeDtypeStruct((b, s, w), F32)),
        grid=(b, nc),
        in_specs=[fwd, fwd, fwd, gfwd, gtfwd, bwd, bwd, bwd, gbwd, gtbwd, _const_spec(bd1.shape)],
        out_specs=(fwd, bwd),
        scratch_shapes=[pltpu.VMEM((n_scan, w, w), F32), pltpu.VMEM((n_scan, 1, w), F32),
                        pltpu.VMEM((n_scan, 1, w), F32)],
        compiler_params=_cparams(("parallel", "arbitrary")),
        name="mlstm_scan",
    )(mq, mk, mv, gate, gate_t, mq, mk, mv, gate, gate_t, bd1)


def _memkv_body(mem_ref, g_ref, w_ref, kg_ref, k_ref, v_ref):
    mb = _rms(mem_ref[0], g_ref[...]).astype(BF16)
    kv = jnp.dot(mb, w_ref[...], preferred_element_type=F32)
    d = kv.shape[1] // 2
    hd = d // XATTN_HEADS
    for h in range(XATTN_HEADS):
        k_ref[0, h * hd:(h + 1) * hd, :] = _rms(kv[:, h * hd:(h + 1) * hd], kg_ref[...]).T.astype(BF16)
    v_ref[0] = kv[:, d:].astype(BF16)


def _memkv(mem, g, wkv, layer, k_gain):
    b, m, d = mem.shape
    spec = pl.BlockSpec((1, m, d), lambda bi: (bi, 0, 0))
    return pl.pallas_call(
        _memkv_body,
        out_shape=(jax.ShapeDtypeStruct((b, d, m), BF16), jax.ShapeDtypeStruct((b, m, d), BF16)),
        grid=(b,),
        in_specs=[spec, _const_spec((1, d)), _layer_spec(wkv.shape[1:], layer), _const_spec(k_gain.shape)],
        out_specs=(pl.BlockSpec((1, d, m), lambda bi: (bi, 0, 0)), spec),
        compiler_params=_cparams(("parallel",)),
        name="memory_kv",
    )(mem, g, wkv, k_gain)


def _mix_out_body(x_ref, ya_ref, yc_ref, hf_ref, hb_ref, og_ref, mg_ref, bd_ref, wo_ref,
                  xg_ref, wq_ref, qg_ref, kt_ref, v_ref, wxo_ref, o_ref):
    hs = hf_ref[0] + hb_ref[0]
    hn = hs * lax.rsqrt(_group_mean(hs * hs, bd_ref[...], split=False) + EPS) * mg_ref[...]
    ym = (og_ref[0] * hn).astype(BF16)
    x = x_ref[0]
    x = (x + jnp.dot(ya_ref[0], wo_ref[0:ATTN_W, :], preferred_element_type=F32)
         + jnp.dot(yc_ref[0], wo_ref[ATTN_W:ATTN_W + CONV_CH, :], preferred_element_type=F32)
         + jnp.dot(ym, wo_ref[ATTN_W + CONV_CH:, :], preferred_element_type=F32))

    xb = _rms(x, xg_ref[...]).astype(BF16)
    qx = jnp.dot(xb, wq_ref[...], preferred_element_type=F32)
    hd = qx.shape[1] // XATTN_HEADS
    outs = []
    for h in range(XATTN_HEADS):
        sl = slice(h * hd, (h + 1) * hd)
        qh = (_rms(qx[:, sl], qg_ref[...]) * (hd ** -0.5)).astype(BF16)
        s = jnp.dot(qh, kt_ref[0, sl, :], preferred_element_type=F32)
        p = jnp.exp(s - jnp.max(s, axis=-1, keepdims=True))
        l = jnp.sum(p, axis=-1, keepdims=True)
        outs.append((jnp.dot(p.astype(BF16), v_ref[0, :, sl], preferred_element_type=F32) / l).astype(BF16))
    o = jnp.concatenate(outs, axis=1)
    o_ref[0] = x + jnp.dot(o, wxo_ref[...], preferred_element_type=F32)


def _mix_out(x, ya, yc, hf, hb, og, m_gain, bd, w_out, x_gain, wq, q_gain, mk, mv, wo, layer, *, tm):
    b, s, d = x.shape
    tok = lambda width: pl.BlockSpec((1, tm, width), lambda bi, i: (bi, i, 0))
    mem_spec = lambda a: pl.BlockSpec((1,) + a.shape[1:], lambda bi, i: (bi, 0, 0))
    return pl.pallas_call(
        _mix_out_body,
        out_shape=jax.ShapeDtypeStruct((b, s, d), F32),
        grid=(b, s // tm),
        in_specs=[tok(d), tok(ATTN_W), tok(CONV_CH), tok(MLSTM_W), tok(MLSTM_W), tok(MLSTM_W),
                  _const_spec(m_gain.shape), _const_spec(bd.shape), _layer_spec(w_out.shape[1:], layer),
                  _const_spec(x_gain.shape), _layer_spec(wq.shape[1:], layer), _const_spec(q_gain.shape),
                  mem_spec(mk), mem_spec(mv), _layer_spec(wo.shape[1:], layer)],
        out_specs=tok(d),
        compiler_params=_cparams(("parallel", "parallel")),
        name="mix_out_cross_attention",
    )(x, ya, yc, hf, hb, og, m_gain, bd, w_out, x_gain, wq, q_gain, mk, mv, wo)


def _rope_tables(s_len):
    rows = s_len // GRID_W
    row_idx = jnp.repeat(jnp.arange(rows, dtype=jnp.int32), GRID_W).astype(F32)
    col_idx = jnp.tile(jnp.arange(GRID_W, dtype=jnp.int32), rows).astype(F32)
    n_freq = HEAD_DIM // 4
    inv_freq = jnp.float32(ROPE_THETA) ** (-jnp.arange(n_freq, dtype=F32) / n_freq)
    ang = jnp.concatenate([row_idx[:, None] * inv_freq, col_idx[:, None] * inv_freq], axis=-1)
    cos = jnp.repeat(jnp.cos(ang), 2, axis=-1)
    sin = jnp.repeat(jnp.sin(ang), 2, axis=-1) * jnp.tile(jnp.array([-1.0, 1.0], F32), HEAD_DIM // 2)
    return jnp.tile(cos, (1, 2)), jnp.tile(sin, (1, 2))


def _block_diag(scale):
    idx = jnp.arange(MLSTM_W) // MLSTM_HEAD_DIM
    return jnp.where(idx[:, None] == idx[None, :], scale, 0.0).astype(BF16)


def _pick_tile(n, pref):
    t = min(n, pref)
    while n % t:
        t //= 2
    return t


def kernel(x, mem, ffn1_norm, ffn1_w13, ffn1_w2, mix_norm, w_in, attn_q_norm, attn_k_norm, conv_dw_w,
           conv_dw_b, conv_ln_g, conv_ln_b, mlstm_gate_b, mlstm_out_norm, w_out, xattn_norm, mem_norm,
           xattn_wq, xattn_wkv, xattn_q_norm, xattn_k_norm, xattn_wo, ffn2_norm, ffn2_w13, ffn2_w2):
    b, s, d = x.shape
    depth = w_in.shape[0]
    cos, sin = _rope_tables(s)
    bd_mean = _block_diag(1.0 / MLSTM_HEAD_DIM)
    bd_sum = _block_diag(1.0)

    tm_ffn = _pick_tile(b * s, 512)
    tm_tok = _pick_tile(s, 1024)
    tq = _pick_tile(s, 512)
    ts = _pick_tile(s, 512)
    row = lambda v: v.reshape(1, -1)

    bf = lambda w: w.astype(BF16)
    ffn1_w13, ffn1_w2, ffn2_w13, ffn2_w2 = bf(ffn1_w13), bf(ffn1_w2), bf(ffn2_w13), bf(ffn2_w2)
    w_out, xattn_wq, xattn_wkv, xattn_wo = bf(w_out), bf(xattn_wq), bf(xattn_wkv), bf(xattn_wo)
    w_gate = w_in[:, :, OFF_MG:]
    w_gate_pad = bf(jnp.pad(w_gate, ((0, 0), (0, 0), (0, LANES - N_GATES))))
    w_gate_t = bf(jnp.swapaxes(w_gate, 1, 2))
    w_in = bf(w_in[:, :, :OFF_MG])

    for l in range(depth):
        x = _ffn(x.reshape(b * s, d), row(ffn1_norm[l]), ffn1_w13, ffn1_w2, l,
                 tm=tm_ffn, n_chunks=FFN_CHUNKS).reshape(b, s, d)

        gate_b = mlstm_gate_b[l].reshape(1, N_GATES)
        q, kk, vt, u, mq, mk, mv, og, gate, gate_t = _inproj(
            x, row(mix_norm[l]), w_in, w_gate_pad, w_gate_t, l,
            jnp.pad(gate_b, ((0, 0), (0, LANES - N_GATES))), gate_b.reshape(N_GATES, 1),
            row(jnp.tile(attn_q_norm[l], ATTN_HEADS)), row(jnp.tile(attn_k_norm[l], ATTN_KV_HEADS)),
            cos, sin, bd_mean, tm=tm_tok)

        ya = _attention(q, kk, vt, tq=tq)
        yc = _conv(u, conv_dw_w[l], row(conv_dw_b[l]), row(conv_ln_g[l]), row(conv_ln_b[l]), ts=ts)
        hf, hb = _mlstm(mq, mk, mv, gate, gate_t, bd_sum)
        mem_k, mem_v = _memkv(mem, row(mem_norm[l]), xattn_wkv, l, row(xattn_k_norm[l]))
        x = _mix_out(x, ya, yc, hf, hb, og, row(mlstm_out_norm[l]), bd_mean, w_out,
                     row(xattn_norm[l]), xattn_wq, row(xattn_q_norm[l]), mem_k, mem_v, xattn_wo, l, tm=tm_tok)

        x = _ffn(x.reshape(b * s, d), row(ffn2_norm[l]), ffn2_w13, ffn2_w2, l,
                 tm=tm_ffn, n_chunks=FFN_CHUNKS).reshape(b, s, d)
    return x
```
